```python
import math
import jax
import jax.numpy as jnp
from jax import lax
import numpy as np

D_MODEL = 1024
BATCH = 8
SEQ = 2048
DEPTH = 1
DEC_BATCH = 32
DEC_SEQ = 32
PAST_LEN = 4096

CHUNK = 64
HEAD_DIM = 64
N_HEADS_DN = 8
N_HEADS_SB = 8
DN_WIDTH = N_HEADS_DN * HEAD_DIM
SB_WIDTH = N_HEADS_SB * HEAD_DIM
MIX_WIDTH = DN_WIDTH + SB_WIDTH
CONV_K = 4
CONV_DIM = 3 * DN_WIDTH
D_FF = 4 * D_MODEL
PLE_DIM = 256
SB_QBLOCK = 128
SB_SCALE = HEAD_DIM ** -0.5
NORM_EPS = 1e-6
SB_OFFSET = CONV_DIM + DN_WIDTH + 2 * N_HEADS_DN
IN_SPLITS = (CONV_DIM, CONV_DIM + DN_WIDTH, CONV_DIM + DN_WIDTH + N_HEADS_DN, SB_OFFSET,
             SB_OFFSET + SB_WIDTH, SB_OFFSET + 2 * SB_WIDTH)
IN_DIM = SB_OFFSET + 3 * SB_WIDTH

kernel_name = "hybrid_deltanet_stickbreak_stream_step"


def rmsnorm(x, g):
    xf = x.astype(jnp.float32)
    xf = xf * lax.rsqrt(jnp.mean(xf * xf, axis=-1, keepdims=True) + NORM_EPS)
    return xf.astype(x.dtype) * g


def l2norm(x):
    xf = x.astype(jnp.float32)
    return xf * lax.rsqrt(jnp.sum(xf * xf, axis=-1, keepdims=True) + NORM_EPS)


def causal_conv(x, w, state):
    t = x.shape[1]
    xp = jnp.concatenate([state.astype(x.dtype), x], axis=1)
    y = xp[:, 0:t] * w[0]
    for i in range(1, CONV_K):
        y = y + xp[:, i:i + t] * w[i]
    return y, xp[:, t:]


def to_chunks(a, n, l):
    b, _, h = a.shape[:3]
    a = a.reshape((b, n, l, h) + a.shape[3:])
    return jnp.moveaxis(a, (1, 3), (0, 2))


def gated_delta_rule(q, k, v, g, beta, s0):
    b, t, h, dk = q.shape
    dv = v.shape[-1]
    l = min(CHUNK, t)
    n = t // l
    qc, kc, vc = to_chunks(q, n, l), to_chunks(k, n, l), to_chunks(v, n, l)
    gc, bc = to_chunks(g, n, l), to_chunks(beta, n, l)
    gcum = jnp.cumsum(gc, axis=-1)
    idx = jnp.arange(l)
    causal = idx[:, None] >= idx[None, :]
    strict = idx[:, None] > idx[None, :]
    decay = jnp.exp(jnp.where(causal, gcum[..., :, None] - gcum[..., None, :], -jnp.inf))
    kb = kc * bc[..., None]
    a_mat = jnp.where(strict, jnp.einsum('nbhid,nbhjd->nbhij', kb, kc) * decay, 0.0)
    rhs = jnp.concatenate([vc * bc[..., None], kb * jnp.exp(gcum)[..., None]], axis=-1)
    sol = lax.linalg.triangular_solve(a_mat + jnp.eye(l, dtype=a_mat.dtype), rhs,
                                      left_side=True, lower=True, unit_diagonal=True)
    u, w = sol[..., :dv], sol[..., dv:]
    qk = jnp.where(causal, jnp.einsum('nbhid,nbhjd->nbhij', qc, kc) * decay, 0.0)

    def step(s, xs):
        q_i, k_i, u_i, w_i, g_i, qk_i = xs
        v_new = u_i - jnp.einsum('bhld,bhde->bhle', w_i, s)
        o = (jnp.einsum('bhld,bhde->bhle', q_i * jnp.exp(g_i)[..., None], s)
             + jnp.einsum('bhij,bhje->bhie', qk_i, v_new))
        g_last = g_i[..., -1]
        s = (s * jnp.exp(g_last)[..., None, None]
             + jnp.einsum('bhld,bhle->bhde', k_i * jnp.exp(g_last[..., None] - g_i)[..., None], v_new))
        return s, o

    s_fin, o = lax.scan(step, s0, (qc, kc, u, w, gcum, qk))
    o = jnp.moveaxis(o, (0, 2), (1, 3)).reshape(b, t, h, dv)
    return o, s_fin


def stick_breaking_block(q, k, v, q_pos, k_pos):
    z = jnp.einsum('bqhd,bshd->bhqs', q, k).astype(jnp.float32) * SB_SCALE
    valid = k_pos[None, :] < q_pos[:, None]
    log_keep = jnp.where(valid, jax.nn.log_sigmoid(-z), 0.0)
    log_rest = lax.cumsum(log_keep, axis=3, reverse=True) - log_keep
    a = jnp.where(valid, jnp.exp(jax.nn.log_sigmoid(z) + log_rest), 0.0)
    return jnp.einsum('bhqs,bshd->bqhd', a.astype(v.dtype), v)


def stick_breaking(q, k, v, q_offset):
    b, t, h, d = q.shape
    qb = min(SB_QBLOCK, t)
    nb = t // qb
    k_pos = jnp.arange(k.shape[1])
    q_pos = (q_offset + jnp.arange(t)).reshape(nb, qb)
    q_blocks = jnp.moveaxis(q.reshape(b, nb, qb, h, d), 1, 0)
    o = lax.map(lambda a: stick_breaking_block(a[0], k, v, a[1], k_pos), (q_blocks, q_pos))
    return jnp.moveaxis(o, 0, 1).reshape(b, t, h, d)


def trunk_layer(x, p, conv_state, s0, k_past, v_past, g_mix, w_in, conv_w, a_log, dt_bias,
                g_out_dn, g_q_sb, g_k_sb, w_out, g_mlp, w_up, w_down, g_ple, w_ple_gate, w_ple_proj):
    b, t, _ = x.shape
    f32 = jnp.float32
    u = rmsnorm(x, g_mix)
    proj = u @ w_in
    dn_qkv, dn_z, dn_alpha, dn_beta, sb_q, sb_k, sb_v = jnp.split(proj, IN_SPLITS, axis=-1)

    c, new_conv = causal_conv(dn_qkv, conv_w, conv_state)
    c = jax.nn.silu(c)
    qa, ka, va = jnp.split(c, 3, axis=-1)
    qa = l2norm(qa.reshape(b, t, N_HEADS_DN, HEAD_DIM)) * (HEAD_DIM ** -0.5)
    ka = l2norm(ka.reshape(b, t, N_HEADS_DN, HEAD_DIM))
    va = va.reshape(b, t, N_HEADS_DN, HEAD_DIM).astype(f32)
    g = -jnp.exp(a_log.astype(f32)) * jax.nn.softplus(dn_alpha.astype(f32) + dt_bias.astype(f32))
    beta = jax.nn.sigmoid(dn_beta.astype(f32))
    o_a, s_new = gated_delta_rule(qa, ka, va, g, beta, s0.astype(f32))
    z = dn_z.reshape(b, t, N_HEADS_DN, HEAD_DIM)
    o_a = (rmsnorm(o_a.astype(x.dtype), g_out_dn) * jax.nn.silu(z)).reshape(b, t, DN_WIDTH)

    qb = rmsnorm(sb_q.reshape(b, t, N_HEADS_SB, HEAD_DIM), g_q_sb)
    kb = rmsnorm(sb_k.reshape(b, t, N_HEADS_SB, HEAD_DIM), g_k_sb)
    vb = sb_v.reshape(b, t, N_HEADS_SB, HEAD_DIM)
    k_all = jnp.concatenate([k_past.astype(kb.dtype), kb], axis=1)
    v_all = jnp.concatenate([v_past.astype(vb.dtype), vb], axis=1)
    o_b = stick_breaking(qb, k_all, v_all, k_past.shape[1]).reshape(b, t, SB_WIDTH)

    h = x + jnp.concatenate([o_a, o_b.astype(o_a.dtype)], axis=-1) @ w_out
    h = h + jnp.square(jax.nn.relu(rmsnorm(h, g_mlp) @ w_up)) @ w_down
    h = h + jax.nn.sigmoid(rmsnorm(h, g_ple) @ w_ple_gate) * (p @ w_ple_proj)
    return h, new_conv, s_new, kb, vb


def setup_inputs(seed: int = 0) -> dict:
    key = jax.random.key(seed)
    ks = jax.random.split(key, 32)
    f32 = jnp.float32

    def nrm(k, shape, scale):
        return jax.random.normal(k, shape, f32) * scale

    def gain(k, shape):
        return 1.0 + 0.05 * jax.random.normal(k, shape, f32)

    dt = jnp.exp(jax.random.uniform(ks[20], (DEPTH, N_HEADS_DN), f32, math.log(1e-3), math.log(1e-1)))
    return {
        "x_prompt": nrm(ks[0], (BATCH, SEQ, D_MODEL), 1.0),
        "x_sample": nrm(ks[1], (DEC_BATCH, DEC_SEQ, D_MODEL), 1.0),
        "cache_conv": nrm(ks[2], (DEPTH, DEC_BATCH, CONV_K - 1, CONV_DIM), 1.0),
        "state_delta": nrm(ks[3], (DEPTH, DEC_BATCH, N_HEADS_DN, HEAD_DIM, HEAD_DIM), 0.3),
        "cache_k": nrm(ks[4], (DEPTH, DEC_BATCH, PAST_LEN, N_HEADS_SB, HEAD_DIM), 1.0),
        "cache_v": nrm(ks[5], (DEPTH, DEC_BATCH, PAST_LEN, N_HEADS_SB, HEAD_DIM), 1.0),
        "p_prompt": nrm(ks[6], (DEPTH, BATCH, SEQ, PLE_DIM), 1.0),
        "p_sample": nrm(ks[7], (DEPTH, DEC_BATCH, DEC_SEQ, PLE_DIM), 1.0),
        "g_mix": gain(ks[8], (DEPTH, D_MODEL)),
        "w_in": nrm(ks[9], (DEPTH, D_MODEL, IN_DIM), D_MODEL ** -0.5),
        "conv_w": nrm(ks[10], (DEPTH, CONV_K, CONV_DIM), CONV_K ** -0.5),
        "a_log": jnp.log(jax.random.uniform(ks[11], (DEPTH, N_HEADS_DN), f32, 1.0, 16.0)),
        "dt_bias": dt + jnp.log(-jnp.expm1(-dt)),
        "g_out_dn": gain(ks[12], (DEPTH, HEAD_DIM)),
        "g_q_sb": gain(ks[13], (DEPTH, HEAD_DIM)),
        "g_k_sb": gain(ks[14], (DEPTH, HEAD_DIM)),
        "w_out": nrm(ks[15], (DEPTH, MIX_WIDTH, D_MODEL), MIX_WIDTH ** -0.5),
        "g_mlp": gain(ks[16], (DEPTH, D_MODEL)),
        "w_up": nrm(ks[17], (DEPTH, D_MODEL, D_FF), D_MODEL ** -0.5),
        "w_down": nrm(ks[18], (DEPTH, D_FF, D_MODEL), D_FF ** -0.5),
        "g_ple": gain(ks[19], (DEPTH, D_MODEL)),
        "w_ple_gate": nrm(ks[21], (DEPTH, D_MODEL, D_MODEL), D_MODEL ** -0.5),
        "w_ple_proj": nrm(ks[22], (DEPTH, PLE_DIM, D_MODEL), PLE_DIM ** -0.5),
    }


def reference(x_prompt, x_sample, cache_conv, state_delta, cache_k, cache_v, p_prompt, p_sample,
              g_mix, w_in, conv_w, a_log, dt_bias, g_out_dn, g_q_sb, g_k_sb, w_out, g_mlp, w_up,
              w_down, g_ple, w_ple_gate, w_ple_proj):
    bp = x_prompt.shape[0]
    y_prompt, y_sample = x_prompt, x_sample
    conv_p, delta_p, k_p, v_p = [], [], [], []
    conv_s, delta_s, k_s, v_s = [], [], [], []
    for i in range(DEPTH):
        wts = (g_mix[i], w_in[i], conv_w[i], a_log[i], dt_bias[i], g_out_dn[i], g_q_sb[i], g_k_sb[i],
               w_out[i], g_mlp[i], w_up[i], w_down[i], g_ple[i], w_ple_gate[i], w_ple_proj[i])
        y_prompt, c, s, k, v = trunk_layer(
            y_prompt, p_prompt[i],
            jnp.zeros((bp, CONV_K - 1, CONV_DIM), x_prompt.dtype),
            jnp.zeros((bp, N_HEADS_DN, HEAD_DIM, HEAD_DIM), jnp.float32),
            jnp.zeros((bp, 0, N_HEADS_SB, HEAD_DIM), x_prompt.dtype),
            jnp.zeros((bp, 0, N_HEADS_SB, HEAD_DIM), x_prompt.dtype),
            *wts)
        conv_p.append(c); delta_p.append(s); k_p.append(k); v_p.append(v)
        y_sample, c, s, k, v = trunk_layer(
            y_sample, p_sample[i], cache_conv[i], state_delta[i], cache_k[i], cache_v[i], *wts)
        conv_s.append(c); delta_s.append(s); k_s.append(k); v_s.append(v)
    return (y_prompt, y_sample,
            jnp.stack(conv_p), jnp.stack(delta_p), jnp.stack(k_p), jnp.stack(v_p),
            jnp.stack(conv_s), jnp.stack(delta_s), jnp.stack(k_s), jnp.stack(v_s))
```

```python
import functools
import math

import jax
import jax.numpy as jnp
from jax import lax
from jax.experimental import pallas as pl
from jax.experimental.pallas import tpu as pltpu

F32 = jnp.float32
BF16 = jnp.bfloat16

D_MODEL = 1024
HEAD_DIM = 64
N_HEADS = 8
WIDTH = N_HEADS * HEAD_DIM
N_PAIRS = N_HEADS // 2
PAIR = 2 * HEAD_DIM
CONV_K = 4
CONV_DIM = 3 * WIDTH
CHUNK = 64
D_FF = 4 * D_MODEL
PLE_DIM = 256
SB_SCALE = HEAD_DIM ** -0.5
NORM_EPS = 1e-6
GATE_PAD = 128
SB_BLOCK = 256
VMEM_LIMIT = 56 * 1024 * 1024


def _dot(a, b):
    return lax.dot_general(a, b, (((1,), (0,)), ((), ())), preferred_element_type=F32)


def _dot_nt(a, b):
    return lax.dot_general(a, b, (((1,), (1,)), ((), ())), preferred_element_type=F32)


def _dot_tn(a, b):
    return lax.dot_general(a, b, (((0,), (0,)), ((), ())), preferred_element_type=F32)


def _split(x, n):
    parts = []
    r = x
    for i in range(n):
        p = r.astype(BF16)
        parts.append(p)
        if i + 1 < n:
            r = r - p.astype(F32)
    return parts


def _mm(dot, a, b, na, nb):
    ap = [a] if a.dtype == BF16 else _split(a, na)
    bp = [b] if b.dtype == BF16 else _split(b, nb)
    out = None
    for i, x in enumerate(ap):
        for j, y in enumerate(bp):
            if i + j >= max(len(ap), len(bp)):
                continue
            t = dot(x, y)
            out = t if out is None else out + t
    return out


def _softplus(x):
    return jnp.maximum(x, 0.0) + jnp.log1p(jnp.exp(-jnp.abs(x)))


def _sigmoid(x):
    return 1.0 / (1.0 + jnp.exp(-x))


def _silu(x):
    return x * _sigmoid(x)


def _const_spec(shape):
    n = len(shape)
    return pl.BlockSpec(shape, lambda *_: (0,) * n)


def _params(sem):
    return pltpu.CompilerParams(dimension_semantics=sem, vmem_limit_bytes=VMEM_LIMIT)


def _head_blockdiag(n, blk):
    r = jnp.arange(n) // blk
    return (r[:, None] == r[None, :]).astype(BF16)


def _inproj_body(x_ref, g_ref, w_ref, wab_ref, bd_ref, gq_ref, gk_ref,
                 qkv_ref, z_ref, ab_ref, q_ref, k_ref, v_ref, k16_ref, v16_ref):
    x = x_ref[...]
    ms = jnp.mean(x * x, axis=-1, keepdims=True)
    u = (x * lax.rsqrt(ms + NORM_EPS) * g_ref[...]).astype(BF16)
    qkv_ref[...] = _dot(u, w_ref[:, 0:CONV_DIM])
    z_ref[...] = _dot(u, w_ref[:, CONV_DIM:CONV_DIM + WIDTH])
    ab_ref[...] = _dot(u, wab_ref[...])
    o = CONV_DIM + WIDTH
    bd = bd_ref[...]

    def headnorm(t, g):
        ss = _mm(_dot, t * t, bd, 2, 1)
        return t * lax.rsqrt(ss * (1.0 / HEAD_DIM) + NORM_EPS) * g

    q = headnorm(_dot(u, w_ref[:, o:o + WIDTH]), gq_ref[...])
    k = headnorm(_dot(u, w_ref[:, o + WIDTH:o + 2 * WIDTH]), gk_ref[...])
    v = _dot(u, w_ref[:, o + 2 * WIDTH:o + 3 * WIDTH])
    q_ref[...] = (q * SB_SCALE).astype(BF16)
    k_ref[...] = k
    v_ref[...] = v
    k16_ref[...] = k.astype(BF16)
    v16_ref[...] = v.astype(BF16)


def _inproj(x2d, g_mix, w_main, w_ab, bd, gq, gk, tm):
    n = x2d.shape[0]
    row = lambda w: pl.BlockSpec((tm, w), lambda i: (i, 0))
    outs = [(CONV_DIM, F32), (WIDTH, F32), (GATE_PAD, F32), (WIDTH, BF16), (WIDTH, F32), (WIDTH, F32),
            (WIDTH, BF16), (WIDTH, BF16)]
    return pl.pallas_call(
        _inproj_body,
        grid=(n // tm,),
        in_specs=[row(D_MODEL), _const_spec((1, D_MODEL)), _const_spec(w_main.shape), _const_spec(w_ab.shape),
                  _const_spec(bd.shape), _const_spec((1, WIDTH)), _const_spec((1, WIDTH))],
        out_specs=[row(w) for w, _ in outs],
        out_shape=[jax.ShapeDtypeStruct((n, w), d) for w, d in outs],
        compiler_params=_params(("parallel",)),
        name="inproj",
    )(x2d, g_mix, w_main, w_ab, bd, gq, gk)


def _deltanet_body(qkv_ref, z_ref, ab_ref, cs_ref, s0_ref, cw_ref, alog_ref, dtb_ref, gout_ref,
                   bd_ref, expand_ref,
                   o_ref, conv_out_ref, s_out_ref,
                   xp_ref, s_ref, *, rows, n_steps):
    c = pl.program_id(1)
    L = CHUNK

    @pl.when(c == 0)
    def _():
        xp_ref[0:8, :] = jnp.zeros((8, CONV_DIM), F32)
        xp_ref[5:8, :] = cs_ref[...]
        s_ref[...] = s0_ref[...]

    xp_ref[8:8 + rows, :] = qkv_ref[...]
    y = xp_ref[5:5 + rows, :] * cw_ref[0:1, :]
    for i in range(1, CONV_K):
        y = y + xp_ref[5 + i:5 + i + rows, :] * cw_ref[i:i + 1, :]
    tail = xp_ref[rows:rows + 8, :]
    xp_ref[0:8, :] = tail

    @pl.when(c == n_steps - 1)
    def _():
        conv_out_ref[...] = tail[5:8, :]

    y = _silu(y)
    ab = ab_ref[...]
    col = lax.broadcasted_iota(jnp.int32, (rows, GATE_PAD), 1)
    g = -jnp.exp(alog_ref[...]) * _softplus(ab + dtb_ref[...])
    gb = jnp.where(col < N_HEADS, g, _sigmoid(ab))
    if rows < L:
        y = jnp.concatenate([y, jnp.zeros((L - rows, CONV_DIM), F32)], axis=0)
        gb = jnp.concatenate([gb, jnp.zeros((L - rows, GATE_PAD), F32)], axis=0)

    bd = bd_ref[...]
    bdf = bd.astype(F32)
    r_i = lax.broadcasted_iota(jnp.int32, (L, L), 0)
    c_i = lax.broadcasted_iota(jnp.int32, (L, L), 1)
    lower = (r_i >= c_i).astype(BF16)
    col64 = lax.broadcasted_iota(jnp.int32, (L, GATE_PAD), 1)
    gcum = jnp.where(col64 < N_HEADS, _mm(_dot, lower, gb, 1, 3), gb)
    ex = _mm(_dot, gcum, expand_ref[...], 3, 1)

    row = lax.broadcasted_iota(jnp.int32, (L, PAIR), 0)
    lane = lax.broadcasted_iota(jnp.int32, (L, PAIR), 1)
    jcol = jnp.where(lane >= HEAD_DIM, lane - HEAD_DIM, lane)
    causal = row >= jcol
    strict = row > jcol
    eye_p = (row == jcol).astype(F32)
    m0 = (lane < HEAD_DIM).astype(F32)
    m1 = 1.0 - m0
    ones_ll = jnp.ones((L, L), BF16)

    def bdiag(x):
        return jnp.concatenate([x * m0, x * m1], axis=0)

    def bdiag2(x):
        mm0 = jnp.concatenate([m0, m0], axis=1)
        return jnp.concatenate([x * mm0, x * (1.0 - mm0)], axis=0)

    mm3 = functools.partial(_mm, na=2, nb=2)

    for p in range(N_PAIRS):
        sl = slice(p * PAIR, (p + 1) * PAIR)
        q = y[:, sl]
        k = y[:, WIDTH + p * PAIR:WIDTH + (p + 1) * PAIR]
        v = y[:, 2 * WIDTH + p * PAIR:2 * WIDTH + (p + 1) * PAIR]
        q = q * lax.rsqrt(_mm(_dot, q * q, bd, 2, 1) + NORM_EPS) * SB_SCALE
        k = k * lax.rsqrt(_mm(_dot, k * k, bd, 2, 1) + NORM_EPS)
        gc = ex[:, sl]
        beta = ex[:, WIDTH + p * PAIR:WIDTH + (p + 1) * PAIR]
        eg = jnp.exp(gc)
        gc_t = _mm(_dot, ones_ll, gc * eye_p, 1, 3)
        decay = jnp.where(causal, jnp.exp(jnp.where(causal, gc - gc_t, 0.0)), 0.0)
        kb = k * beta
        kst = bdiag(k)
        a = jnp.where(strict, mm3(_dot_nt, kb, kst) * decay, 0.0)
        qk = jnp.where(causal, mm3(_dot_nt, q, kst) * decay, 0.0)
        x = jnp.concatenate([v * beta, kb * eg], axis=1)
        x = x - mm3(_dot, a, bdiag2(x))
        ak = a
        for _ in range(5):
            ak = mm3(_dot, ak, bdiag(ak))
            x = x + mm3(_dot, ak, bdiag2(x))
        u = x[:, :PAIR]
        w = x[:, PAIR:]
        s = s_ref[p]
        v_new = u - mm3(_dot, w, s)
        o = mm3(_dot, q * eg, s) + mm3(_dot, qk, bdiag(v_new))
        g_last = gc[L - 1:L, :]
        s_ref[p] = s * jnp.exp(g_last) + mm3(_dot_tn, k * jnp.exp(g_last - gc), v_new) * bdf
        o = o * lax.rsqrt(_mm(_dot, o * o, bd, 2, 1) * (1.0 / HEAD_DIM) + NORM_EPS) * gout_ref[:, sl]
        o_ref[:, sl] = o[:rows] * _silu(z_ref[:, sl])

    @pl.when(c == n_steps - 1)
    def _():
        s_out_ref[...] = s_ref[...]


def _deltanet(qkv, z, ab, conv_state, s0_bd, conv_w, alog, dtb, gout, bd_pair, expand, batch, seq):
    rows = min(CHUNK, seq)
    n_steps = seq // rows
    tok = lambda w: pl.BlockSpec((rows, w), lambda b, c: (b * n_steps + c, 0))
    per_b3 = lambda s: pl.BlockSpec((None,) + s, lambda b, c: (b, 0, 0))
    per_b4 = lambda s: pl.BlockSpec((None,) + s, lambda b, c: (b, 0, 0, 0))
    body = functools.partial(_deltanet_body, rows=rows, n_steps=n_steps)
    return pl.pallas_call(
        body,
        grid=(batch, n_steps),
        in_specs=[tok(CONV_DIM), tok(WIDTH), tok(GATE_PAD),
                  per_b3((CONV_K - 1, CONV_DIM)), per_b4((N_PAIRS, PAIR, PAIR)),
                  _const_spec((CONV_K, CONV_DIM)), _const_spec((1, GATE_PAD)), _const_spec((1, GATE_PAD)),
                  _const_spec((1, WIDTH)), _const_spec((PAIR, PAIR)), _const_spec(expand.shape)],
        out_specs=[tok(WIDTH), per_b3((CONV_K - 1, CONV_DIM)), per_b4((N_PAIRS, PAIR, PAIR))],
        out_shape=[jax.ShapeDtypeStruct((batch * seq, WIDTH), F32),
                   jax.ShapeDtypeStruct((batch, CONV_K - 1, CONV_DIM), F32),
                   jax.ShapeDtypeStruct((batch, N_PAIRS, PAIR, PAIR), F32)],
        scratch_shapes=[pltpu.VMEM((rows + 8, CONV_DIM), F32), pltpu.VMEM((N_PAIRS, PAIR, PAIR), F32)],
        compiler_params=_params(("parallel", "arbitrary")),
        name="deltanet",
    )(qkv, z, ab, conv_state, s0_bd, conv_w, alog, dtb, gout, bd_pair, expand)


def _sb_block(q, k, v, carry, upper, valid):
    z = _dot_nt(q, k)
    sp = _softplus(z)
    log_keep = -sp
    if valid is not None:
        log_keep = jnp.where(valid, log_keep, 0.0)
    log_rest = _mm(_dot, log_keep, upper, 2, 1) + carry
    a = jnp.exp(z - sp + log_rest)
    if valid is not None:
        a = jnp.where(valid, a, 0.0)
    pv = _dot(a.astype(BF16), v)
    return pv, carry + jnp.sum(log_keep, axis=1, keepdims=True)


def _upper(n):
    r = lax.broadcasted_iota(jnp.int32, (n, n), 0)
    c = lax.broadcasted_iota(jnp.int32, (n, n), 1)
    return (r > c).astype(BF16)


def _sb_prompt_body(q_ref, k_ref, v_ref, o_ref):
    i = pl.program_id(1)
    t = SB_BLOCK
    upper = _upper(t)
    r = lax.broadcasted_iota(jnp.int32, (t, t), 0)
    c = lax.broadcasted_iota(jnp.int32, (t, t), 1)
    strict = c < r
    lane = lax.broadcasted_iota(jnp.int32, (t, PAIR), 1)
    halves = (lane < HEAD_DIM, lane >= HEAD_DIM)
    for p in range(N_PAIRS):
        sl = slice(p * PAIR, (p + 1) * PAIR)
        accs = []
        for hmask in halves:
            q = jnp.where(hmask, q_ref[:, sl], jnp.zeros((), BF16))
            start = pl.multiple_of(i * t, t)
            pv, carry = _sb_block(q, k_ref[pl.ds(start, t), sl], v_ref[pl.ds(start, t), sl],
                                  jnp.zeros((t, 1), F32), upper, strict)

            def step(n, st):
                acc, carry = st
                off = pl.multiple_of((i - 1 - n) * t, t)
                pv, carry = _sb_block(q, k_ref[pl.ds(off, t), sl], v_ref[pl.ds(off, t), sl],
                                      carry, upper, None)
                return acc + pv, carry

            acc, _ = lax.fori_loop(0, i, step, (pv, carry))
            accs.append(acc)
        o_ref[:, sl] = jnp.where(halves[0], accs[0], accs[1])


def _sb_prompt(q16, k16, v16, batch, seq):
    t = SB_BLOCK
    nq = seq // t
    q3 = q16.reshape(batch, seq, WIDTH)
    k3 = k16.reshape(batch, seq, WIDTH)
    v3 = v16.reshape(batch, seq, WIDTH)
    full = pl.BlockSpec((None, seq, WIDTH), lambda b, i: (b, 0, 0))
    blk = pl.BlockSpec((None, t, WIDTH), lambda b, i: (b, i, 0))
    out = pl.pallas_call(
        _sb_prompt_body,
        grid=(batch, nq),
        in_specs=[blk, full, full],
        out_specs=blk,
        out_shape=jax.ShapeDtypeStruct((batch, seq, WIDTH), F32),
        compiler_params=_params(("parallel", "arbitrary")),
        name="sb_prompt",
    )(q3, k3, v3)
    return out.reshape(batch * seq, WIDTH)


def _sb_sample_body(q_ref, kn_ref, vn_ref, kp_ref, vp_ref, o_ref, qs_ref, acc_ref, carry_ref, *, tq, tk, n_steps):
    j = pl.program_id(1)
    t = SB_BLOCK
    m = N_HEADS * tq
    upper = _upper(t)

    @pl.when(j == 0)
    def _():
        q = q_ref[...]
        lane = lax.broadcasted_iota(jnp.int32, (tq, WIDTH), 1)
        for h in range(N_HEADS):
            hm = (lane >= h * HEAD_DIM) & (lane < (h + 1) * HEAD_DIM)
            qs_ref[h * tq:(h + 1) * tq, :] = jnp.where(hm, q, jnp.zeros((), BF16))
        kn = jnp.concatenate([kn_ref[...], jnp.zeros((t - tq, WIDTH), BF16)], axis=0)
        vn = jnp.concatenate([vn_ref[...], jnp.zeros((t - tq, WIDTH), BF16)], axis=0)
        r = lax.broadcasted_iota(jnp.int32, (m, t), 0)
        c = lax.broadcasted_iota(jnp.int32, (m, t), 1)
        valid = c < (r & (tq - 1))
        pv, carry = _sb_block(qs_ref[...], kn, vn, jnp.zeros((m, 1), F32), upper, valid)
        acc_ref[...] = pv
        carry_ref[...] = carry

    qs = qs_ref[...]
    acc = acc_ref[...]
    carry = carry_ref[...]
    for s in reversed(range(tk // t)):
        pv, carry = _sb_block(qs, kp_ref[s * t:(s + 1) * t, :].astype(BF16),
                              vp_ref[s * t:(s + 1) * t, :].astype(BF16), carry, upper, None)
        acc = acc + pv
    acc_ref[...] = acc
    carry_ref[...] = carry

    @pl.when(j == n_steps - 1)
    def _():
        lane = lax.broadcasted_iota(jnp.int32, (tq, WIDTH), 1)
        out = jnp.zeros((tq, WIDTH), F32)
        for h in range(N_HEADS):
            hm = (lane >= h * HEAD_DIM) & (lane < (h + 1) * HEAD_DIM)
            out = out + jnp.where(hm, acc[h * tq:(h + 1) * tq, :], 0.0)
        o_ref[...] = out


def _sb_sample(q16, k16, v16, past_k, past_v, batch, tq, tk):
    past = past_k.shape[1]
    n_steps = past // tk
    assert tq & (tq - 1) == 0 and tq <= SB_BLOCK
    new = lambda: pl.BlockSpec((None, tq, WIDTH), lambda b, j: (b, 0, 0))
    old = lambda: pl.BlockSpec((None, tk, WIDTH), lambda b, j: (b, n_steps - 1 - j, 0))
    m = N_HEADS * tq
    body = functools.partial(_sb_sample_body, tq=tq, tk=tk, n_steps=n_steps)
    out = pl.pallas_call(
        body,
        grid=(batch, n_steps),
        in_specs=[new(), new(), new(), old(), old()],
        out_specs=new(),
        out_shape=jax.ShapeDtypeStruct((batch, tq, WIDTH), F32),
        scratch_shapes=[pltpu.VMEM((m, WIDTH), BF16), pltpu.VMEM((m, WIDTH), F32), pltpu.VMEM((m, 1), F32)],
        compiler_params=_params(("parallel", "arbitrary")),
        name="sb_sample",
    )(q16.reshape(batch, tq, WIDTH), k16.reshape(batch, tq, WIDTH), v16.reshape(batch, tq, WIDTH),
      past_k, past_v)
    return out.reshape(batch * tq, WIDTH)


def _post_body(x_ref, oa_ref, ob_ref, p_ref, wo_ref, gm_ref, wu_ref, wd_ref, gp_ref, wg_ref, wp_ref, y_ref):
    def rms(h, g):
        ms = jnp.mean(h * h, axis=-1, keepdims=True)
        return (h * lax.rsqrt(ms + NORM_EPS) * g).astype(BF16)

    y_ref[...] = x_ref[...] + _dot(oa_ref[...].astype(BF16), wo_ref[0:WIDTH, :]) \
        + _dot(ob_ref[...].astype(BF16), wo_ref[WIDTH:2 * WIDTH, :])
    h = y_ref[...]
    u = rms(h, gm_ref[...])
    ff = D_MODEL
    for j in range(D_FF // ff):
        up = jnp.maximum(_dot(u, wu_ref[:, j * ff:(j + 1) * ff]), 0.0)
        h = h + _dot((up * up).astype(BF16), wd_ref[j * ff:(j + 1) * ff, :])
    y_ref[...] = h
    h = y_ref[...]
    gate = _sigmoid(_dot(rms(h, gp_ref[...]), wg_ref[...]))
    y_ref[...] = h + gate * _dot(p_ref[...].astype(BF16), wp_ref[...])


def _post(x2d, oa, ob, p2d, w_out, g_mlp, w_up, w_down, g_ple, w_gate, w_proj, tm):
    n = x2d.shape[0]
    row = lambda w: pl.BlockSpec((tm, w), lambda i: (i, 0))
    cs = lambda a: pl.BlockSpec(a.shape, lambda i: (0, 0), pipeline_mode=pl.Buffered(1))
    return pl.pallas_call(
        _post_body,
        grid=(n // tm,),
        in_specs=[row(D_MODEL), row(WIDTH), row(WIDTH), row(PLE_DIM), cs(w_out), cs(g_mlp), cs(w_up), cs(w_down),
                  cs(g_ple), cs(w_gate), cs(w_proj)],
        out_specs=row(D_MODEL),
        out_shape=jax.ShapeDtypeStruct((n, D_MODEL), F32),
        compiler_params=_params(("parallel",)),
        name="post",
    )(x2d, oa, ob, p2d, w_out, g_mlp, w_up, w_down, g_ple, w_gate, w_proj)


def _pair_blockdiag(s):
    b = s.shape[0]
    s = s.reshape(b, N_PAIRS, 2, HEAD_DIM, HEAD_DIM)
    zero = jnp.zeros_like(s[:, :, 0])
    top = jnp.concatenate([s[:, :, 0], zero], axis=-1)
    bot = jnp.concatenate([zero, s[:, :, 1]], axis=-1)
    return jnp.concatenate([top, bot], axis=-2)


def _pair_unblock(s):
    b = s.shape[0]
    h0 = s[:, :, :HEAD_DIM, :HEAD_DIM]
    h1 = s[:, :, HEAD_DIM:, HEAD_DIM:]
    return jnp.stack([h0, h1], axis=2).reshape(b, N_HEADS, HEAD_DIM, HEAD_DIM)


def _expand_matrix():
    lane = jnp.arange(2 * WIDTH)
    src = (lane // WIDTH) * N_HEADS + (lane % WIDTH) // HEAD_DIM
    return (jnp.arange(GATE_PAD)[:, None] == src[None, :]).astype(BF16)


def _group(x, p, conv_state, s0, past_k, past_v, wts, tm, sb_tk):
    batch, seq, _ = x.shape
    n = batch * seq
    x2d = x.reshape(n, D_MODEL)
    qkv, z, ab, q16, k, v, k16, v16 = _inproj(x2d, wts["g_mix"], wts["w_main"], wts["w_ab"], wts["bd_w"],
                                              wts["gq"], wts["gk"], tm)
    o_a, new_conv, s_new = _deltanet(qkv, z, ab, conv_state, _pair_blockdiag(s0), wts["conv_w"], wts["alog"],
                                     wts["dtb"], wts["gout"], wts["bd_pair"], wts["expand"], batch, seq)
    if past_k is None:
        o_b = _sb_prompt(q16, k16, v16, batch, seq)
    else:
        o_b = _sb_sample(q16, k16, v16, past_k, past_v, batch, seq, sb_tk)
    y = _post(x2d, o_a, o_b, p.reshape(n, PLE_DIM), wts["w_out"], wts["g_mlp"], wts["w_up"], wts["w_down"],
              wts["g_ple"], wts["w_gate"], wts["w_proj"], tm)
    return (y.reshape(batch, seq, D_MODEL), new_conv, _pair_unblock(s_new),
            k.reshape(batch, seq, N_HEADS, HEAD_DIM), v.reshape(batch, seq, N_HEADS, HEAD_DIM))


def _layer_weights(i, g_mix, w_in, conv_w, a_log, dt_bias, g_out_dn, g_q_sb, g_k_sb, w_out, g_mlp, w_up, w_down,
                   g_ple, w_ple_gate, w_ple_proj):
    w = w_in[i]
    gate0 = CONV_DIM + WIDTH
    sb0 = gate0 + 2 * N_HEADS
    pad_row = lambda a: jnp.pad(a.astype(F32), (0, GATE_PAD - a.shape[0])).reshape(1, GATE_PAD)
    return {
        "g_mix": g_mix[i].reshape(1, D_MODEL),
        "w_main": jnp.concatenate([w[:, :gate0], w[:, sb0:]], axis=1).astype(BF16),
        "w_ab": jnp.pad(w[:, gate0:sb0], ((0, 0), (0, GATE_PAD - 2 * N_HEADS))).astype(BF16),
        "bd_w": _head_blockdiag(WIDTH, HEAD_DIM),
        "bd_pair": _head_blockdiag(PAIR, HEAD_DIM),
        "expand": _expand_matrix(),
        "gq": jnp.tile(g_q_sb[i], N_HEADS).reshape(1, WIDTH),
        "gk": jnp.tile(g_k_sb[i], N_HEADS).reshape(1, WIDTH),
        "conv_w": conv_w[i],
        "alog": pad_row(a_log[i]),
        "dtb": pad_row(dt_bias[i]),
        "gout": jnp.tile(g_out_dn[i], N_HEADS).reshape(1, WIDTH),
        "w_out": w_out[i].astype(BF16),
        "g_mlp": g_mlp[i].reshape(1, D_MODEL),
        "w_up": w_up[i].astype(BF16),
        "w_down": w_down[i].astype(BF16),
        "g_ple": g_ple[i].reshape(1, D_MODEL),
        "w_gate": w_ple_gate[i].astype(BF16),
        "w_proj": w_ple_proj[i].astype(BF16),
    }


def kernel(x_prompt, x_sample, cache_conv, state_delta, cache_k, cache_v, p_prompt, p_sample, g_mix, w_in, conv_w, a_log, dt_bias, g_out_dn, g_q_sb, g_k_sb, w_out, g_mlp, w_up, w_down, g_ple, w_ple_gate, w_ple_proj):
    depth = w_in.shape[0]
    bp = x_prompt.shape[0]
    y_p, y_s = x_prompt, x_sample
    outs_p, outs_s = [], []
    for i in range(depth):
        wts = _layer_weights(i, g_mix, w_in, conv_w, a_log, dt_bias, g_out_dn, g_q_sb, g_k_sb, w_out, g_mlp,
                             w_up, w_down, g_ple, w_ple_gate, w_ple_proj)
        y_p, *rest = _group(y_p, p_prompt[i], jnp.zeros((bp, CONV_K - 1, CONV_DIM), F32),
                            jnp.zeros((bp, N_HEADS, HEAD_DIM, HEAD_DIM), F32), None, None, wts,
                            tm=512, sb_tk=None)
        outs_p.append(rest)
        dec_b, past_len = cache_k.shape[1], cache_k.shape[2]
        y_s, *rest = _group(y_s, p_sample[i], cache_conv[i], state_delta[i],
                            cache_k[i].reshape(dec_b, past_len, WIDTH), cache_v[i].reshape(dec_b, past_len, WIDTH),
                            wts, tm=512, sb_tk=min(1024, past_len))
        outs_s.append(rest)
    stack = lambda outs, j: jnp.stack([o[j] for o in outs])
    return (y_p, y_s,
            stack(outs_p, 0), stack(outs_p, 1), stack(outs_p, 2), stack(outs_p, 3),
            stack(outs_s, 0), stack(outs_s, 1), stack(outs_s, 2), stack(outs_s, 3))
```

```python
import functools
import math

import jax
import jax.numpy as jnp
from jax import lax
from jax.experimental import pallas as pl
from jax.experimental.pallas import tpu as pltpu

F32 = jnp.float32
BF16 = jnp.bfloat16

D_MODEL = 1024
HEAD_DIM = 64
N_HEADS = 8
WIDTH = N_HEADS * HEAD_DIM
N_PAIRS = N_HEADS // 2
PAIR = 2 * HEAD_DIM
CONV_K = 4
CONV_DIM = 3 * WIDTH
CHUNK = 64
D_FF = 4 * D_MODEL
PLE_DIM = 256
SB_SCALE = HEAD_DIM ** -0.5
NORM_EPS = 1e-6
GATE_PAD = 128
SB_BLOCK = 256
VMEM_LIMIT = 56 * 1024 * 1024


def _dot(a, b):
    return lax.dot_general(a, b, (((1,), (0,)), ((), ())), preferred_element_type=F32)


def _dot_nt(a, b):
    return lax.dot_general(a, b, (((1,), (1,)), ((), ())), preferred_element_type=F32)


def _dot_tn(a, b):
    return lax.dot_general(a, b, (((0,), (0,)), ((), ())), preferred_element_type=F32)


def _split(x, n):
    parts = []
    r = x
    for i in range(n):
        p = r.astype(BF16)
        parts.append(p)
        if i + 1 < n:
            r = r - p.astype(F32)
    return parts


def _mm(dot, a, b, na, nb):
    ap = [a] if a.dtype == BF16 else _split(a, na)
    bp = [b] if b.dtype == BF16 else _split(b, nb)
    out = None
    for i, x in enumerate(ap):
        for j, y in enumerate(bp):
            if i + j >= max(len(ap), len(bp)):
                continue
            t = dot(x, y)
            out = t if out is None else out + t
    return out


def _softplus(x):
    return jnp.maximum(x, 0.0) + jnp.log1p(jnp.exp(-jnp.abs(x)))


def _sigmoid(x):
    return 1.0 / (1.0 + jnp.exp(-x))


def _silu(x):
    return x * _sigmoid(x)


def _const_spec(shape):
    n = len(shape)
    return pl.BlockSpec(shape, lambda *_: (0,) * n)


def _params(sem):
    return pltpu.CompilerParams(dimension_semantics=sem, vmem_limit_bytes=VMEM_LIMIT)


def _head_blockdiag(n, blk):
    r = jnp.arange(n) // blk
    return (r[:, None] == r[None, :]).astype(BF16)


def _inproj_body(x_ref, g_ref, w_ref, wab_ref, bd_ref, gq_ref, gk_ref,
                 qkv_ref, z_ref, ab_ref, q_ref, k_ref, v_ref, k16_ref, v16_ref):
    x = x_ref[...]
    ms = jnp.mean(x * x, axis=-1, keepdims=True)
    u = (x * lax.rsqrt(ms + NORM_EPS) * g_ref[...]).astype(BF16)
    qkv_ref[...] = _dot(u, w_ref[:, 0:CONV_DIM])
    z_ref[...] = _dot(u, w_ref[:, CONV_DIM:CONV_DIM + WIDTH])
    ab_ref[...] = _dot(u, wab_ref[...])
    o = CONV_DIM + WIDTH
    bd = bd_ref[...]

    def headnorm(t, g):
        ss = _mm(_dot, t * t, bd, 2, 1)
        return t * lax.rsqrt(ss * (1.0 / HEAD_DIM) + NORM_EPS) * g

    q = headnorm(_dot(u, w_ref[:, o:o + WIDTH]), gq_ref[...])
    k = headnorm(_dot(u, w_ref[:, o + WIDTH:o + 2 * WIDTH]), gk_ref[...])
    v = _dot(u, w_ref[:, o + 2 * WIDTH:o + 3 * WIDTH])
    q_ref[...] = (q * SB_SCALE).astype(BF16)
    k_ref[...] = k
    v_ref[...] = v
    k16_ref[...] = k.astype(BF16)
    v16_ref[...] = v.astype(BF16)


def _inproj(x2d, g_mix, w_main, w_ab, bd, gq, gk, tm):
    n = x2d.shape[0]
    row = lambda w: pl.BlockSpec((tm, w), lambda i: (i, 0))
    outs = [(CONV_DIM, F32), (WIDTH, F32), (GATE_PAD, F32), (WIDTH, BF16), (WIDTH, F32), (WIDTH, F32),
            (WIDTH, BF16), (WIDTH, BF16)]
    return pl.pallas_call(
        _inproj_body,
        grid=(n // tm,),
        in_specs=[row(D_MODEL), _const_spec((1, D_MODEL)), _const_spec(w_main.shape), _const_spec(w_ab.shape),
                  _const_spec(bd.shape), _const_spec((1, WIDTH)), _const_spec((1, WIDTH))],
        out_specs=[row(w) for w, _ in outs],
        out_shape=[jax.ShapeDtypeStruct((n, w), d) for w, d in outs],
        compiler_params=_params(("parallel",)),
        name="inproj",
    )(x2d, g_mix, w_main, w_ab, bd, gq, gk)


DN_CHUNKS = 4
DN_STREAMS = 8
DN_PRECISE_FACTORS = 3


def _lane_head0(shape):
    return lax.broadcasted_iota(jnp.int32, shape, len(shape) - 1) % PAIR < HEAD_DIM


def _bdiag(x):
    m0 = _lane_head0(x.shape)
    zero = jnp.zeros((), x.dtype)
    return jnp.concatenate([jnp.where(m0, x, zero), jnp.where(m0, zero, x)], axis=0)


def _dn_prep_body(qkv_ref, halo_ref, ab_ref, cs_ref, cw_ref, alog_ref, dtb_ref, bdw_ref, expand_ref,
                  u_ref, w_ref, qe_ref, ks_ref, qk_ref, egl_ref,
                  xp_ref, y_ref, gb_ref, *, seg_rows, n_seg, blocks_per_stream):
    L = CHUNK
    pad = max(seg_rows, L)
    R = n_seg * pad
    nc = R // L

    for s in range(n_seg):
        if blocks_per_stream:
            xp_ref[s, 5:8, :] = halo_ref[5:8, :]

            @pl.when(pl.program_id(0) % blocks_per_stream == 0)
            def _():
                xp_ref[s, 5:8, :] = cs_ref[...]
        else:
            xp_ref[s, 5:8, :] = cs_ref[s]
        xp_ref[s, 8:8 + seg_rows, :] = qkv_ref[s * seg_rows:(s + 1) * seg_rows, :]
        y = xp_ref[s, 5:5 + seg_rows, :] * cw_ref[0:1, :]
        for i in range(1, CONV_K):
            y = y + xp_ref[s, 5 + i:5 + i + seg_rows, :] * cw_ref[i:i + 1, :]
        y_ref[s * pad:s * pad + seg_rows, :] = _silu(y)
        ab = ab_ref[s * seg_rows:(s + 1) * seg_rows, :]
        col = lax.broadcasted_iota(jnp.int32, ab.shape, 1)
        g = -jnp.exp(alog_ref[...]) * _softplus(ab + dtb_ref[...])
        gb_ref[s * pad:s * pad + seg_rows, :] = jnp.where(col < N_HEADS, g, _sigmoid(ab))
        if seg_rows < pad:
            y_ref[s * pad + seg_rows:(s + 1) * pad, :] = jnp.zeros((pad - seg_rows, CONV_DIM), F32)
            gb_ref[s * pad + seg_rows:(s + 1) * pad, :] = jnp.zeros((pad - seg_rows, GATE_PAD), F32)

    bdw = bdw_ref[...]
    q = y_ref[:, 0:WIDTH]
    k = y_ref[:, WIDTH:2 * WIDTH]
    q = q * lax.rsqrt(_mm(_dot, q * q, bdw, 2, 1) + NORM_EPS) * SB_SCALE
    k = k * lax.rsqrt(_mm(_dot, k * k, bdw, 2, 1) + NORM_EPS)

    rr = lax.broadcasted_iota(jnp.int32, (R, R), 0)
    cc = lax.broadcasted_iota(jnp.int32, (R, R), 1)
    same = (rr // L) == (cc // L)
    blk_lower = (same & (rr >= cc)).astype(BF16)
    blk_ones = same.astype(BF16)
    gb = gb_ref[...]
    colg = lax.broadcasted_iota(jnp.int32, (R, GATE_PAD), 1)
    gcum = jnp.where(colg < N_HEADS, _mm(_dot, blk_lower, gb, 1, 3), gb)
    ex = _mm(_dot, gcum, expand_ref[...], 3, 1)
    gc_all = ex[:, 0:WIDTH]
    beta_all = ex[:, WIDTH:2 * WIDTH]
    r512 = lax.broadcasted_iota(jnp.int32, (R, WIDTH), 0)
    l512 = lax.broadcasted_iota(jnp.int32, (R, WIDTH), 1)
    eye = (r512 % L) == (l512 % HEAD_DIM)
    gct_all = _mm(_dot, blk_ones, jnp.where(eye, gc_all, 0.0), 1, 2)

    row = lax.broadcasted_iota(jnp.int32, (L, PAIR), 0)
    jcol = lax.broadcasted_iota(jnp.int32, (L, PAIR), 1) % HEAD_DIM
    causal = row >= jcol
    strict = row > jcol

    chains = [(slice(c * L, (c + 1) * L), slice(p * PAIR, (p + 1) * PAIR)) for c in range(nc) for p in range(N_PAIRS)]
    for c in range(nc):
        egl_ref[c] = jnp.exp(gc_all[(c + 1) * L - 1:(c + 1) * L, :])
    a_list, x_list = [], []
    for rs, sl in chains:
        qc = q[rs, sl]
        kc = k[rs, sl]
        vc = y_ref[rs, 2 * WIDTH + sl.start:2 * WIDTH + sl.stop]
        gc = gc_all[rs, sl]
        beta = beta_all[rs, sl]
        eg = jnp.exp(gc)
        decay = jnp.where(causal, jnp.exp(jnp.where(causal, gc - gct_all[rs, sl], 0.0)), 0.0)
        kb = kc * beta
        kst = _bdiag(kc.astype(BF16))
        aq = _dot_nt(jnp.concatenate([kb, qc], axis=0).astype(BF16), kst)
        a_list.append(jnp.where(strict, aq[0:L] * decay, 0.0))
        x_list.append(jnp.concatenate([vc * beta, kb * eg], axis=1))
        qk_ref[rs, sl] = jnp.where(causal, aq[L:2 * L] * decay, 0.0).astype(BF16)
        qe_ref[rs, sl] = (qc * eg).astype(BF16)
        ks_ref[rs, sl] = (kc * jnp.exp(gc[L - 1:L, :] - gc)).astype(BF16)
    for t in range(6):
        np_ = 2 if t < DN_PRECISE_FACTORS else 1
        for n in range(len(chains)):
            ak, x = a_list[n], x_list[n]
            if np_ == 1:
                ak, x = ak.astype(BF16), x.astype(BF16)
            if t < 5:
                r = _mm(_dot, ak, jnp.concatenate([_bdiag(ak), _bdiag(x)], axis=1), np_, np_)
                a_list[n], ax = r[:, 0:PAIR], r[:, PAIR:]
            else:
                ax = _mm(_dot, ak, _bdiag(x), np_, np_)
            x_list[n] = x_list[n] - ax if t == 0 else x_list[n] + ax
    for (rs, sl), x in zip(chains, x_list):
        u_ref[rs, sl] = x[:, 0:PAIR]
        w_ref[rs, sl] = x[:, PAIR:].astype(BF16)


def _dn_prep(qkv, ab, conv_state, wts, n_streams, seq):
    L = CHUNK
    if seq >= L:
        seg_rows, n_seg = DN_CHUNKS * L, 1
        blocks_per_stream = seq // seg_rows
        cs_spec = pl.BlockSpec((None, CONV_K - 1, CONV_DIM), lambda i: (i // blocks_per_stream, 0, 0))
    else:
        seg_rows, n_seg, blocks_per_stream = seq, DN_CHUNKS, 0
        cs_spec = pl.BlockSpec((n_seg, CONV_K - 1, CONV_DIM), lambda i: (i, 0, 0))
    rows_in = n_seg * seg_rows
    R = n_seg * max(seg_rows, L)
    n_blocks = n_streams * seq // rows_in
    tok = lambda w: pl.BlockSpec((rows_in, w), lambda i: (i, 0))
    halo = pl.BlockSpec((8, CONV_DIM), lambda i: (jnp.maximum(i * (rows_in // 8) - 1, 0), 0))
    out = lambda: pl.BlockSpec((R, WIDTH), lambda i: (i, 0))
    n_rows = n_blocks * R
    body = functools.partial(_dn_prep_body, seg_rows=seg_rows, n_seg=n_seg, blocks_per_stream=blocks_per_stream)
    return pl.pallas_call(
        body,
        grid=(n_blocks,),
        in_specs=[tok(CONV_DIM), halo, tok(GATE_PAD), cs_spec,
                  _const_spec((CONV_K, CONV_DIM)), _const_spec((1, GATE_PAD)), _const_spec((1, GATE_PAD)),
                  _const_spec((WIDTH, WIDTH)), _const_spec((GATE_PAD, 2 * WIDTH))],
        out_specs=[out(), out(), out(), out(), out(), pl.BlockSpec((R // L, 1, WIDTH), lambda i: (i, 0, 0))],
        out_shape=[jax.ShapeDtypeStruct((n_rows, WIDTH), F32)] +
                  [jax.ShapeDtypeStruct((n_rows, WIDTH), BF16)] * 4 +
                  [jax.ShapeDtypeStruct((n_rows // L, 1, WIDTH), F32)],
        scratch_shapes=[pltpu.VMEM((n_seg, seg_rows + 8, CONV_DIM), F32), pltpu.VMEM((R, CONV_DIM), F32),
                        pltpu.VMEM((R, GATE_PAD), F32)],
        compiler_params=_params(("parallel",)),
        name="dn_prep",
    )(qkv, qkv, ab, conv_state, wts["conv_w"], wts["alog"], wts["dtb"], wts["bd_w"], wts["expand"])


def _dn_scan_body(u_ref, w_ref, qe_ref, ks_ref, qk_ref, egl_ref, z_ref, s0_ref, gout_ref, bdw_ref,
                  o_ref, s_out_ref, s_ref, oraw_ref, *, rows, nb, n_steps):
    c = pl.program_id(1)
    L = CHUNK

    @pl.when(c == 0)
    def _():
        s_ref[...] = s0_ref[...]

    r = lax.broadcasted_iota(jnp.int32, (PAIR, PAIR), 0)
    cl = lax.broadcasted_iota(jnp.int32, (PAIR, PAIR), 1)
    same_head = (r // HEAD_DIM) == (cl // HEAD_DIM)
    chains = [(b, p, slice(p * PAIR, (p + 1) * PAIR)) for b in range(nb) for p in range(N_PAIRS)]
    group = 4 * N_PAIRS
    for g0 in range(0, len(chains), group):
        grp = chains[g0:g0 + group]
        ws = [_dot(jnp.concatenate([w_ref[b, :, sl], qe_ref[b, :, sl]], axis=0), s_ref[b, p].astype(BF16))
              for b, p, sl in grp]
        v_new = [(u_ref[b, :, sl] - r[0:L]).astype(BF16) for (b, p, sl), r in zip(grp, ws)]
        for (b, p, sl), r, v in zip(grp, ws, v_new):
            oraw_ref[b, :, sl] = r[L:2 * L] + _dot(qk_ref[b, :, sl], _bdiag(v))
        for (b, p, sl), v in zip(grp, v_new):
            s_ref[b, p] = (s_ref[b, p] * egl_ref[b, 0, :, sl]
                           + jnp.where(same_head, _dot_tn(ks_ref[b, :, sl], v), 0.0))

    o = oraw_ref[...].reshape(nb * L, WIDTH)
    o = o * lax.rsqrt(_mm(_dot, o * o, bdw_ref[...], 2, 1) * (1.0 / HEAD_DIM) + NORM_EPS) * gout_ref[...]
    for b in range(nb):
        o_ref[b] = o[b * L:b * L + rows] * _silu(z_ref[b])

    @pl.when(c == n_steps - 1)
    def _():
        s_out_ref[...] = s_ref[...]


def _dn_scan(prep, z, s0_bd, wts, n_streams, seq):
    L = CHUNK
    rows = min(L, seq)
    n_steps = seq // rows
    nb = DN_STREAMS
    u, w, qe, ks, qk, egl = prep
    as3 = lambda a: a.reshape(n_streams, n_steps * L, WIDTH)
    blk = lambda: pl.BlockSpec((nb, L, WIDTH), lambda g, c: (g, c, 0))
    tok = lambda: pl.BlockSpec((nb, rows, WIDTH), lambda g, c: (g, c, 0))
    st = lambda: pl.BlockSpec((nb, N_PAIRS, PAIR, PAIR), lambda g, c: (g, 0, 0, 0))
    body = functools.partial(_dn_scan_body, rows=rows, nb=nb, n_steps=n_steps)
    o, s_new = pl.pallas_call(
        body,
        grid=(n_streams // nb, n_steps),
        in_specs=[blk(), blk(), blk(), blk(), blk(),
                  pl.BlockSpec((nb, 1, 1, WIDTH), lambda g, c: (g, c, 0, 0)), tok(), st(),
                  _const_spec((1, WIDTH)), _const_spec((WIDTH, WIDTH))],
        out_specs=[tok(), st()],
        out_shape=[jax.ShapeDtypeStruct((n_streams, seq, WIDTH), F32),
                   jax.ShapeDtypeStruct((n_streams, N_PAIRS, PAIR, PAIR), F32)],
        scratch_shapes=[pltpu.VMEM((nb, N_PAIRS, PAIR, PAIR), F32), pltpu.VMEM((nb, L, WIDTH), F32)],
        compiler_params=_params(("parallel", "arbitrary")),
        name="dn_scan",
    )(as3(u), as3(w), as3(qe), as3(ks), as3(qk), egl.reshape(n_streams, n_steps, 1, WIDTH),
      z.reshape(n_streams, seq, WIDTH), s0_bd, wts["gout"], wts["bd_w"])
    return o.reshape(n_streams * seq, WIDTH), s_new


def _sb_block(q, k, v, carry, upper, valid):
    z = _dot_nt(q, k)
    sp = _softplus(z)
    log_keep = -sp
    if valid is not None:
        log_keep = jnp.where(valid, log_keep, 0.0)
    log_rest = _mm(_dot, log_keep, upper, 2, 1) + carry
    a = jnp.exp(z - sp + log_rest)
    if valid is not None:
        a = jnp.where(valid, a, 0.0)
    pv = _dot(a.astype(BF16), v)
    return pv, carry + jnp.sum(log_keep, axis=1, keepdims=True)


def _upper(n):
    r = lax.broadcasted_iota(jnp.int32, (n, n), 0)
    c = lax.broadcasted_iota(jnp.int32, (n, n), 1)
    return (r > c).astype(BF16)


def _sb_prompt_body(q_ref, k_ref, v_ref, o_ref):
    i = pl.program_id(1)
    t = SB_BLOCK
    upper = _upper(t)
    r = lax.broadcasted_iota(jnp.int32, (t, t), 0)
    c = lax.broadcasted_iota(jnp.int32, (t, t), 1)
    strict = c < r
    lane = lax.broadcasted_iota(jnp.int32, (t, PAIR), 1)
    halves = (lane < HEAD_DIM, lane >= HEAD_DIM)
    for p in range(N_PAIRS):
        sl = slice(p * PAIR, (p + 1) * PAIR)
        accs = []
        for hmask in halves:
            q = jnp.where(hmask, q_ref[:, sl], jnp.zeros((), BF16))
            start = pl.multiple_of(i * t, t)
            pv, carry = _sb_block(q, k_ref[pl.ds(start, t), sl], v_ref[pl.ds(start, t), sl],
                                  jnp.zeros((t, 1), F32), upper, strict)

            def step(n, st):
                acc, carry = st
                off = pl.multiple_of((i - 1 - n) * t, t)
                pv, carry = _sb_block(q, k_ref[pl.ds(off, t), sl], v_ref[pl.ds(off, t), sl],
                                      carry, upper, None)
                return acc + pv, carry

            acc, _ = lax.fori_loop(0, i, step, (pv, carry))
            accs.append(acc)
        o_ref[:, sl] = jnp.where(halves[0], accs[0], accs[1])


def _sb_prompt(q16, k16, v16, batch, seq):
    t = SB_BLOCK
    nq = seq // t
    q3 = q16.reshape(batch, seq, WIDTH)
    k3 = k16.reshape(batch, seq, WIDTH)
    v3 = v16.reshape(batch, seq, WIDTH)
    full = pl.BlockSpec((None, seq, WIDTH), lambda b, i: (b, 0, 0))
    blk = pl.BlockSpec((None, t, WIDTH), lambda b, i: (b, i, 0))
    out = pl.pallas_call(
        _sb_prompt_body,
        grid=(batch, nq),
        in_specs=[blk, full, full],
        out_specs=blk,
        out_shape=jax.ShapeDtypeStruct((batch, seq, WIDTH), F32),
        compiler_params=_params(("parallel", "arbitrary")),
        name="sb_prompt",
    )(q3, k3, v3)
    return out.reshape(batch * seq, WIDTH)


def _sb_sample_body(q_ref, kn_ref, vn_ref, kp_ref, vp_ref, o_ref, qs_ref, acc_ref, carry_ref, *, tq, tk, n_steps):
    j = pl.program_id(1)
    t = SB_BLOCK
    m = N_HEADS * tq
    upper = _upper(t)

    @pl.when(j == 0)
    def _():
        q = q_ref[...]
        lane = lax.broadcasted_iota(jnp.int32, (tq, WIDTH), 1)
        for h in range(N_HEADS):
            hm = (lane >= h * HEAD_DIM) & (lane < (h + 1) * HEAD_DIM)
            qs_ref[h * tq:(h + 1) * tq, :] = jnp.where(hm, q, jnp.zeros((), BF16))
        kn = jnp.concatenate([kn_ref[...], jnp.zeros((t - tq, WIDTH), BF16)], axis=0)
        vn = jnp.concatenate([vn_ref[...], jnp.zeros((t - tq, WIDTH), BF16)], axis=0)
        r = lax.broadcasted_iota(jnp.int32, (m, t), 0)
        c = lax.broadcasted_iota(jnp.int32, (m, t), 1)
        valid = c < (r & (tq - 1))
        pv, carry = _sb_block(qs_ref[...], kn, vn, jnp.zeros((m, 1), F32), upper, valid)
        acc_ref[...] = pv
        carry_ref[...] = carry

    qs = qs_ref[...]
    acc = acc_ref[...]
    carry = carry_ref[...]
    for s in reversed(range(tk // t)):
        pv, carry = _sb_block(qs, kp_ref[s * t:(s + 1) * t, :].astype(BF16),
                              vp_ref[s * t:(s + 1) * t, :].astype(BF16), carry, upper, None)
        acc = acc + pv
    acc_ref[...] = acc
    carry_ref[...] = carry

    @pl.when(j == n_steps - 1)
    def _():
        lane = lax.broadcasted_iota(jnp.int32, (tq, WIDTH), 1)
        out = jnp.zeros((tq, WIDTH), F32)
        for h in range(N_HEADS):
            hm = (lane >= h * HEAD_DIM) & (lane < (h + 1) * HEAD_DIM)
            out = out + jnp.where(hm, acc[h * tq:(h + 1) * tq, :], 0.0)
        o_ref[...] = out


def _sb_sample(q16, k16, v16, past_k, past_v, batch, tq, tk):
    past = past_k.shape[1]
    n_steps = past // tk
    assert tq & (tq - 1) == 0 and tq <= SB_BLOCK
    new = lambda: pl.BlockSpec((None, tq, WIDTH), lambda b, j: (b, 0, 0))
    old = lambda: pl.BlockSpec((None, tk, WIDTH), lambda b, j: (b, n_steps - 1 - j, 0))
    m = N_HEADS * tq
    body = functools.partial(_sb_sample_body, tq=tq, tk=tk, n_steps=n_steps)
    out = pl.pallas_call(
        body,
        grid=(batch, n_steps),
        in_specs=[new(), new(), new(), old(), old()],
        out_specs=new(),
        out_shape=jax.ShapeDtypeStruct((batch, tq, WIDTH), F32),
        scratch_shapes=[pltpu.VMEM((m, WIDTH), BF16), pltpu.VMEM((m, WIDTH), F32), pltpu.VMEM((m, 1), F32)],
        compiler_params=_params(("parallel", "arbitrary")),
        name="sb_sample",
    )(q16.reshape(batch, tq, WIDTH), k16.reshape(batch, tq, WIDTH), v16.reshape(batch, tq, WIDTH),
      past_k, past_v)
    return out.reshape(batch * tq, WIDTH)


def _post_body(x_ref, oa_ref, ob_ref, p_ref, wo_ref, gm_ref, wu_ref, wd_ref, gp_ref, wg_ref, wp_ref, y_ref):
    def rms(h, g):
        ms = jnp.mean(h * h, axis=-1, keepdims=True)
        return (h * lax.rsqrt(ms + NORM_EPS) * g).astype(BF16)

    y_ref[...] = x_ref[...] + _dot(oa_ref[...].astype(BF16), wo_ref[0:WIDTH, :]) \
        + _dot(ob_ref[...].astype(BF16), wo_ref[WIDTH:2 * WIDTH, :])
    h = y_ref[...]
    u = rms(h, gm_ref[...])
    ff = D_MODEL
    for j in range(D_FF // ff):
        up = jnp.maximum(_dot(u, wu_ref[:, j * ff:(j + 1) * ff]), 0.0)
        h = h + _dot((up * up).astype(BF16), wd_ref[j * ff:(j + 1) * ff, :])
    y_ref[...] = h
    h = y_ref[...]
    gate = _sigmoid(_dot(rms(h, gp_ref[...]), wg_ref[...]))
    y_ref[...] = h + gate * _dot(p_ref[...].astype(BF16), wp_ref[...])


def _post(x2d, oa, ob, p2d, w_out, g_mlp, w_up, w_down, g_ple, w_gate, w_proj, tm):
    n = x2d.shape[0]
    row = lambda w: pl.BlockSpec((tm, w), lambda i: (i, 0))
    cs = lambda a: pl.BlockSpec(a.shape, lambda i: (0, 0), pipeline_mode=pl.Buffered(1))
    return pl.pallas_call(
        _post_body,
        grid=(n // tm,),
        in_specs=[row(D_MODEL), row(WIDTH), row(WIDTH), row(PLE_DIM), cs(w_out), cs(g_mlp), cs(w_up), cs(w_down),
                  cs(g_ple), cs(w_gate), cs(w_proj)],
        out_specs=row(D_MODEL),
        out_shape=jax.ShapeDtypeStruct((n, D_MODEL), F32),
        compiler_params=_params(("parallel",)),
        name="post",
    )(x2d, oa, ob, p2d, w_out, g_mlp, w_up, w_down, g_ple, w_gate, w_proj)


def _pair_blockdiag(s):
    b = s.shape[0]
    s = s.reshape(b, N_PAIRS, 2, HEAD_DIM, HEAD_DIM)
    zero = jnp.zeros_like(s[:, :, 0])
    top = jnp.concatenate([s[:, :, 0], zero], axis=-1)
    bot = jnp.concatenate([zero, s[:, :, 1]], axis=-1)
    return jnp.concatenate([top, bot], axis=-2)


def _pair_unblock(s):
    b = s.shape[0]
    h0 = s[:, :, :HEAD_DIM, :HEAD_DIM]
    h1 = s[:, :, HEAD_DIM:, HEAD_DIM:]
    return jnp.stack([h0, h1], axis=2).reshape(b, N_HEADS, HEAD_DIM, HEAD_DIM)


def _expand_matrix():
    lane = jnp.arange(2 * WIDTH)
    src = (lane // WIDTH) * N_HEADS + (lane % WIDTH) // HEAD_DIM
    return (jnp.arange(GATE_PAD)[:, None] == src[None, :]).astype(BF16)


def _group(x, p, conv_state, s0, past_k, past_v, wts, tm, sb_tk):
    batch, seq, _ = x.shape
    n = batch * seq
    x2d = x.reshape(n, D_MODEL)
    qkv, z, ab, q16, k, v, k16, v16 = _inproj(x2d, wts["g_mix"], wts["w_main"], wts["w_ab"], wts["bd_w"],
                                              wts["gq"], wts["gk"], tm)
    prep = _dn_prep(qkv, ab, conv_state, wts, batch, seq)
    o_a, s_new = _dn_scan(prep, z, _pair_blockdiag(s0), wts, batch, seq)
    new_conv = qkv.reshape(batch, seq, CONV_DIM)[:, seq - (CONV_K - 1):, :]
    if past_k is None:
        o_b = _sb_prompt(q16, k16, v16, batch, seq)
    else:
        o_b = _sb_sample(q16, k16, v16, past_k, past_v, batch, seq, sb_tk)
    y = _post(x2d, o_a, o_b, p.reshape(n, PLE_DIM), wts["w_out"], wts["g_mlp"], wts["w_up"], wts["w_down"],
              wts["g_ple"], wts["w_gate"], wts["w_proj"], tm)
    return (y.reshape(batch, seq, D_MODEL), new_conv, _pair_unblock(s_new),
            k.reshape(batch, seq, N_HEADS, HEAD_DIM), v.reshape(batch, seq, N_HEADS, HEAD_DIM))


def _layer_weights(i, g_mix, w_in, conv_w, a_log, dt_bias, g_out_dn, g_q_sb, g_k_sb, w_out, g_mlp, w_up, w_down,
                   g_ple, w_ple_gate, w_ple_proj):
    w = w_in[i]
    gate0 = CONV_DIM + WIDTH
    sb0 = gate0 + 2 * N_HEADS
    pad_row = lambda a: jnp.pad(a.astype(F32), (0, GATE_PAD - a.shape[0])).reshape(1, GATE_PAD)
    return {
        "g_mix": g_mix[i].reshape(1, D_MODEL),
        "w_main": jnp.concatenate([w[:, :gate0], w[:, sb0:]], axis=1).astype(BF16),
        "w_ab": jnp.pad(w[:, gate0:sb0], ((0, 0), (0, GATE_PAD - 2 * N_HEADS))).astype(BF16),
        "bd_w": _head_blockdiag(WIDTH, HEAD_DIM),
        "expand": _expand_matrix(),
        "gq": jnp.tile(g_q_sb[i], N_HEADS).reshape(1, WIDTH),
        "gk": jnp.tile(g_k_sb[i], N_HEADS).reshape(1, WIDTH),
        "conv_w": conv_w[i],
        "alog": pad_row(a_log[i]),
        "dtb": pad_row(dt_bias[i]),
        "gout": jnp.tile(g_out_dn[i], N_HEADS).reshape(1, WIDTH),
        "w_out": w_out[i].astype(BF16),
        "g_mlp": g_mlp[i].reshape(1, D_MODEL),
        "w_up": w_up[i].astype(BF16),
        "w_down": w_down[i].astype(BF16),
        "g_ple": g_ple[i].reshape(1, D_MODEL),
        "w_gate": w_ple_gate[i].astype(BF16),
        "w_proj": w_ple_proj[i].astype(BF16),
    }


def kernel(x_prompt, x_sample, cache_conv, state_delta, cache_k, cache_v, p_prompt, p_sample, g_mix, w_in, conv_w, a_log, dt_bias, g_out_dn, g_q_sb, g_k_sb, w_out, g_mlp, w_up, w_down, g_ple, w_ple_gate, w_ple_proj):
    depth = w_in.shape[0]
    bp = x_prompt.shape[0]
    y_p, y_s = x_prompt, x_sample
    outs_p, outs_s = [], []
    for i in range(depth):
        wts = _layer_weights(i, g_mix, w_in, conv_w, a_log, dt_bias, g_out_dn, g_q_sb, g_k_sb, w_out, g_mlp,
                             w_up, w_down, g_ple, w_ple_gate, w_ple_proj)
        y_p, *rest = _group(y_p, p_prompt[i], jnp.zeros((bp, CONV_K - 1, CONV_DIM), F32),
                            jnp.zeros((bp, N_HEADS, HEAD_DIM, HEAD_DIM), F32), None, None, wts,
                            tm=512, sb_tk=None)
        outs_p.append(rest)
        dec_b, past_len = cache_k.shape[1], cache_k.shape[2]
        y_s, *rest = _group(y_s, p_sample[i], cache_conv[i], state_delta[i],
                            cache_k[i].reshape(dec_b, past_len, WIDTH), cache_v[i].reshape(dec_b, past_len, WIDTH),
                            wts, tm=512, sb_tk=min(1024, past_len))
        outs_s.append(rest)
    stack = lambda outs, j: jnp.stack([o[j] for o in outs])
    return (y_p, y_s,
            stack(outs_p, 0), stack(outs_p, 1), stack(outs_p, 2), stack(outs_p, 3),
            stack(outs_s, 0), stack(outs_s, 1), stack(outs_s, 2), stack(outs_s, 3))
```

```python
import functools
import math

import jax
import jax.numpy as jnp
from jax import lax
from jax.experimental import pallas as pl
from jax.experimental.pallas import tpu as pltpu

F32 = jnp.float32
BF16 = jnp.bfloat16

D_MODEL = 1024
HEAD_DIM = 64
N_HEADS = 8
WIDTH = N_HEADS * HEAD_DIM
N_PAIRS = N_HEADS // 2
PAIR = 2 * HEAD_DIM
CONV_K = 4
CONV_DIM = 3 * WIDTH
CHUNK = 64
D_FF = 4 * D_MODEL
PLE_DIM = 256
SB_SCALE = HEAD_DIM ** -0.5
NORM_EPS = 1e-6
GATE_PAD = 128
SB_BLOCK = 256
SB_PAIRS = 4
VMEM_LIMIT = 56 * 1024 * 1024


def _dot(a, b):
    return lax.dot_general(a, b, (((1,), (0,)), ((), ())), preferred_element_type=F32)


def _dot_nt(a, b):
    return lax.dot_general(a, b, (((1,), (1,)), ((), ())), preferred_element_type=F32)


def _dot_tn(a, b):
    return lax.dot_general(a, b, (((0,), (0,)), ((), ())), preferred_element_type=F32)


def _split(x, n):
    parts = []
    r = x
    for i in range(n):
        p = r.astype(BF16)
        parts.append(p)
        if i + 1 < n:
            r = r - p.astype(F32)
    return parts


def _mm(dot, a, b, na, nb):
    ap = [a] if a.dtype == BF16 else _split(a, na)
    bp = [b] if b.dtype == BF16 else _split(b, nb)
    out = None
    for i, x in enumerate(ap):
        for j, y in enumerate(bp):
            if i + j >= max(len(ap), len(bp)):
                continue
            t = dot(x, y)
            out = t if out is None else out + t
    return out


def _softplus(x):
    return jnp.maximum(x, 0.0) + jnp.log1p(jnp.exp(-jnp.abs(x)))


def _sigmoid(x):
    return 1.0 / (1.0 + jnp.exp(-x))


def _silu(x):
    return x * _sigmoid(x)


def _const_spec(shape):
    n = len(shape)
    return pl.BlockSpec(shape, lambda *_: (0,) * n)


def _params(sem):
    return pltpu.CompilerParams(dimension_semantics=sem, vmem_limit_bytes=VMEM_LIMIT)


def _head_blockdiag(n, blk):
    r = jnp.arange(n) // blk
    return (r[:, None] == r[None, :]).astype(BF16)


def _inproj_body(x_ref, g_ref, w_ref, wab_ref, bd_ref, gq_ref, gk_ref,
                 qkv_ref, z_ref, ab_ref, q_ref, k_ref, v_ref, k16_ref, v16_ref):
    x = x_ref[...]
    ms = jnp.mean(x * x, axis=-1, keepdims=True)
    u = (x * lax.rsqrt(ms + NORM_EPS) * g_ref[...]).astype(BF16)
    qkv_ref[...] = _dot(u, w_ref[:, 0:CONV_DIM])
    z_ref[...] = _dot(u, w_ref[:, CONV_DIM:CONV_DIM + WIDTH])
    ab_ref[...] = _dot(u, wab_ref[...])
    o = CONV_DIM + WIDTH
    bd = bd_ref[...]

    def headnorm(t, g):
        ss = _mm(_dot, t * t, bd, 2, 1)
        return t * lax.rsqrt(ss * (1.0 / HEAD_DIM) + NORM_EPS) * g

    q = headnorm(_dot(u, w_ref[:, o:o + WIDTH]), gq_ref[...])
    k = headnorm(_dot(u, w_ref[:, o + WIDTH:o + 2 * WIDTH]), gk_ref[...])
    v = _dot(u, w_ref[:, o + 2 * WIDTH:o + 3 * WIDTH])
    q_ref[...] = (q * SB_SCALE).astype(BF16)
    tm = k.shape[0]
    for h in range(N_HEADS):
        k_ref[pl.ds(h, tm, stride=N_HEADS), :] = k[:, h * HEAD_DIM:(h + 1) * HEAD_DIM]
        v_ref[pl.ds(h, tm, stride=N_HEADS), :] = v[:, h * HEAD_DIM:(h + 1) * HEAD_DIM]
    k16_ref[...] = k.astype(BF16)
    v16_ref[...] = v.astype(BF16)


def _inproj(x2d, g_mix, w_main, w_ab, bd, gq, gk, tm):
    n = x2d.shape[0]
    row = lambda w: pl.BlockSpec((tm, w), lambda i: (i, 0))
    heads = pl.BlockSpec((tm * N_HEADS, HEAD_DIM), lambda i: (i, 0))
    outs = [(n, CONV_DIM, F32, row), (n, WIDTH, F32, row), (n, GATE_PAD, F32, row), (n, WIDTH, BF16, row),
            (n * N_HEADS, HEAD_DIM, F32, None), (n * N_HEADS, HEAD_DIM, F32, None),
            (n, WIDTH, BF16, row), (n, WIDTH, BF16, row)]
    return pl.pallas_call(
        _inproj_body,
        grid=(n // tm,),
        in_specs=[row(D_MODEL), _const_spec((1, D_MODEL)), _const_spec(w_main.shape), _const_spec(w_ab.shape),
                  _const_spec(bd.shape), _const_spec((1, WIDTH)), _const_spec((1, WIDTH))],
        out_specs=[heads if mk is None else mk(w) for _, w, _, mk in outs],
        out_shape=[jax.ShapeDtypeStruct((r, w), d) for r, w, d, _ in outs],
        compiler_params=_params(("parallel",)),
        name="inproj",
    )(x2d, g_mix, w_main, w_ab, bd, gq, gk)


DN_CHUNKS = 4
DN_STREAMS = 8
DN_PRECISE_FACTORS = 3


def _lane_head0(shape):
    return lax.broadcasted_iota(jnp.int32, shape, len(shape) - 1) % PAIR < HEAD_DIM


def _bdiag(x):
    m0 = _lane_head0(x.shape)
    zero = jnp.zeros((), x.dtype)
    return jnp.concatenate([jnp.where(m0, x, zero), jnp.where(m0, zero, x)], axis=0)


def _dn_prep_body(qkv_ref, halo_ref, ab_ref, cs_ref, cw_ref, alog_ref, dtb_ref, bdw_ref, expand_ref,
                  u_ref, w_ref, qe_ref, ks_ref, qk_ref, egl_ref,
                  xp_ref, y_ref, gb_ref, *, seg_rows, n_seg, blocks_per_stream):
    L = CHUNK
    pad = max(seg_rows, L)
    R = n_seg * pad
    nc = R // L

    for s in range(n_seg):
        if blocks_per_stream:
            xp_ref[s, 5:8, :] = halo_ref[5:8, :]

            @pl.when(pl.program_id(0) % blocks_per_stream == 0)
            def _():
                xp_ref[s, 5:8, :] = cs_ref[...]
        else:
            xp_ref[s, 5:8, :] = cs_ref[s]
        xp_ref[s, 8:8 + seg_rows, :] = qkv_ref[s * seg_rows:(s + 1) * seg_rows, :]
        y = xp_ref[s, 5:5 + seg_rows, :] * cw_ref[0:1, :]
        for i in range(1, CONV_K):
            y = y + xp_ref[s, 5 + i:5 + i + seg_rows, :] * cw_ref[i:i + 1, :]
        y_ref[s * pad:s * pad + seg_rows, :] = _silu(y)
        ab = ab_ref[s * seg_rows:(s + 1) * seg_rows, :]
        col = lax.broadcasted_iota(jnp.int32, ab.shape, 1)
        g = -jnp.exp(alog_ref[...]) * _softplus(ab + dtb_ref[...])
        gb_ref[s * pad:s * pad + seg_rows, :] = jnp.where(col < N_HEADS, g, _sigmoid(ab))
        if seg_rows < pad:
            y_ref[s * pad + seg_rows:(s + 1) * pad, :] = jnp.zeros((pad - seg_rows, CONV_DIM), F32)
            gb_ref[s * pad + seg_rows:(s + 1) * pad, :] = jnp.zeros((pad - seg_rows, GATE_PAD), F32)

    bdw = bdw_ref[...]
    q = y_ref[:, 0:WIDTH]
    k = y_ref[:, WIDTH:2 * WIDTH]
    q = q * lax.rsqrt(_mm(_dot, q * q, bdw, 2, 1) + NORM_EPS) * SB_SCALE
    k = k * lax.rsqrt(_mm(_dot, k * k, bdw, 2, 1) + NORM_EPS)

    rr = lax.broadcasted_iota(jnp.int32, (R, R), 0)
    cc = lax.broadcasted_iota(jnp.int32, (R, R), 1)
    same = (rr // L) == (cc // L)
    blk_lower = (same & (rr >= cc)).astype(BF16)
    blk_ones = same.astype(BF16)
    gb = gb_ref[...]
    colg = lax.broadcasted_iota(jnp.int32, (R, GATE_PAD), 1)
    gcum = jnp.where(colg < N_HEADS, _mm(_dot, blk_lower, gb, 1, 3), gb)
    ex = _mm(_dot, gcum, expand_ref[...], 3, 1)
    gc_all = ex[:, 0:WIDTH]
    beta_all = ex[:, WIDTH:2 * WIDTH]
    r512 = lax.broadcasted_iota(jnp.int32, (R, WIDTH), 0)
    l512 = lax.broadcasted_iota(jnp.int32, (R, WIDTH), 1)
    eye = (r512 % L) == (l512 % HEAD_DIM)
    gct_all = _mm(_dot, blk_ones, jnp.where(eye, gc_all, 0.0), 1, 2)

    row = lax.broadcasted_iota(jnp.int32, (L, PAIR), 0)
    jcol = lax.broadcasted_iota(jnp.int32, (L, PAIR), 1) % HEAD_DIM
    causal = row >= jcol
    strict = row > jcol

    chains = [(slice(c * L, (c + 1) * L), slice(p * PAIR, (p + 1) * PAIR)) for c in range(nc) for p in range(N_PAIRS)]
    for c in range(nc):
        egl_ref[c] = jnp.exp(gc_all[(c + 1) * L - 1:(c + 1) * L, :])
    a_list, x_list = [], []
    for rs, sl in chains:
        qc = q[rs, sl]
        kc = k[rs, sl]
        vc = y_ref[rs, 2 * WIDTH + sl.start:2 * WIDTH + sl.stop]
        gc = gc_all[rs, sl]
        beta = beta_all[rs, sl]
        eg = jnp.exp(gc)
        decay = jnp.where(causal, jnp.exp(jnp.where(causal, gc - gct_all[rs, sl], 0.0)), 0.0)
        kb = kc * beta
        kst = _bdiag(kc.astype(BF16))
        aq = _dot_nt(jnp.concatenate([kb, qc], axis=0).astype(BF16), kst)
        a_list.append(jnp.where(strict, aq[0:L] * decay, 0.0))
        x_list.append(jnp.concatenate([vc * beta, kb * eg], axis=1))
        qk_ref[rs, sl] = jnp.where(causal, aq[L:2 * L] * decay, 0.0).astype(BF16)
        qe_ref[rs, sl] = (qc * eg).astype(BF16)
        ks_ref[rs, sl] = (kc * jnp.exp(gc[L - 1:L, :] - gc)).astype(BF16)
    for t in range(6):
        np_ = 2 if t < DN_PRECISE_FACTORS else 1
        for n in range(len(chains)):
            ak, x = a_list[n], x_list[n]
            if np_ == 1:
                ak, x = ak.astype(BF16), x.astype(BF16)
            if t < 5:
                r = _mm(_dot, ak, jnp.concatenate([_bdiag(ak), _bdiag(x)], axis=1), np_, np_)
                a_list[n], ax = r[:, 0:PAIR], r[:, PAIR:]
            else:
                ax = _mm(_dot, ak, _bdiag(x), np_, np_)
            x_list[n] = x_list[n] - ax if t == 0 else x_list[n] + ax
    for (rs, sl), x in zip(chains, x_list):
        u_ref[rs, sl] = x[:, 0:PAIR]
        w_ref[rs, sl] = x[:, PAIR:].astype(BF16)


def _dn_prep(qkv, ab, conv_state, wts, n_streams, seq):
    L = CHUNK
    if seq >= L:
        seg_rows, n_seg = DN_CHUNKS * L, 1
        blocks_per_stream = seq // seg_rows
        cs_spec = pl.BlockSpec((None, CONV_K - 1, CONV_DIM), lambda i: (i // blocks_per_stream, 0, 0))
    else:
        seg_rows, n_seg, blocks_per_stream = seq, DN_CHUNKS, 0
        cs_spec = pl.BlockSpec((n_seg, CONV_K - 1, CONV_DIM), lambda i: (i, 0, 0))
    rows_in = n_seg * seg_rows
    R = n_seg * max(seg_rows, L)
    n_blocks = n_streams * seq // rows_in
    tok = lambda w: pl.BlockSpec((rows_in, w), lambda i: (i, 0))
    halo = pl.BlockSpec((8, CONV_DIM), lambda i: (jnp.maximum(i * (rows_in // 8) - 1, 0), 0))
    out = lambda: pl.BlockSpec((R, WIDTH), lambda i: (i, 0))
    n_rows = n_blocks * R
    body = functools.partial(_dn_prep_body, seg_rows=seg_rows, n_seg=n_seg, blocks_per_stream=blocks_per_stream)
    return pl.pallas_call(
        body,
        grid=(n_blocks,),
        in_specs=[tok(CONV_DIM), halo, tok(GATE_PAD), cs_spec,
                  _const_spec((CONV_K, CONV_DIM)), _const_spec((1, GATE_PAD)), _const_spec((1, GATE_PAD)),
                  _const_spec((WIDTH, WIDTH)), _const_spec((GATE_PAD, 2 * WIDTH))],
        out_specs=[out(), out(), out(), out(), out(), pl.BlockSpec((R // L, 1, WIDTH), lambda i: (i, 0, 0))],
        out_shape=[jax.ShapeDtypeStruct((n_rows, WIDTH), F32)] +
                  [jax.ShapeDtypeStruct((n_rows, WIDTH), BF16)] * 4 +
                  [jax.ShapeDtypeStruct((n_rows // L, 1, WIDTH), F32)],
        scratch_shapes=[pltpu.VMEM((n_seg, seg_rows + 8, CONV_DIM), F32), pltpu.VMEM((R, CONV_DIM), F32),
                        pltpu.VMEM((R, GATE_PAD), F32)],
        compiler_params=_params(("parallel",)),
        name="dn_prep",
    )(qkv, qkv, ab, conv_state, wts["conv_w"], wts["alog"], wts["dtb"], wts["bd_w"], wts["expand"])


def _dn_scan_body(u_ref, w_ref, qe_ref, ks_ref, qk_ref, egl_ref, z_ref, s0_ref, gout_ref, bdw_ref,
                  o_ref, s_out_ref, s_ref, oraw_ref, *, rows, nb, n_steps):
    c = pl.program_id(1)
    L = CHUNK

    @pl.when(c == 0)
    def _():
        s_ref[...] = s0_ref[...]

    r = lax.broadcasted_iota(jnp.int32, (PAIR, PAIR), 0)
    cl = lax.broadcasted_iota(jnp.int32, (PAIR, PAIR), 1)
    same_head = (r // HEAD_DIM) == (cl // HEAD_DIM)
    chains = [(b, p, slice(p * PAIR, (p + 1) * PAIR)) for b in range(nb) for p in range(N_PAIRS)]
    group = 4 * N_PAIRS
    for g0 in range(0, len(chains), group):
        grp = chains[g0:g0 + group]
        ws = [_dot(jnp.concatenate([w_ref[b, :, sl], qe_ref[b, :, sl]], axis=0), s_ref[b, p].astype(BF16))
              for b, p, sl in grp]
        v_new = [(u_ref[b, :, sl] - r[0:L]).astype(BF16) for (b, p, sl), r in zip(grp, ws)]
        for (b, p, sl), r, v in zip(grp, ws, v_new):
            oraw_ref[b, :, sl] = r[L:2 * L] + _dot(qk_ref[b, :, sl], _bdiag(v))
        for (b, p, sl), v in zip(grp, v_new):
            s_ref[b, p] = (s_ref[b, p] * egl_ref[b, 0, :, sl]
                           + jnp.where(same_head, _dot_tn(ks_ref[b, :, sl], v), 0.0))

    o = oraw_ref[...].reshape(nb * L, WIDTH)
    o = o * lax.rsqrt(_mm(_dot, o * o, bdw_ref[...], 2, 1) * (1.0 / HEAD_DIM) + NORM_EPS) * gout_ref[...]
    for b in range(nb):
        o_ref[b] = o[b * L:b * L + rows] * _silu(z_ref[b])

    @pl.when(c == n_steps - 1)
    def _():
        s_out_ref[...] = s_ref[...]


def _dn_scan(prep, z, s0_bd, wts, n_streams, seq):
    L = CHUNK
    rows = min(L, seq)
    n_steps = seq // rows
    nb = DN_STREAMS
    u, w, qe, ks, qk, egl = prep
    as3 = lambda a: a.reshape(n_streams, n_steps * L, WIDTH)
    blk = lambda: pl.BlockSpec((nb, L, WIDTH), lambda g, c: (g, c, 0))
    tok = lambda: pl.BlockSpec((nb, rows, WIDTH), lambda g, c: (g, c, 0))
    st = lambda: pl.BlockSpec((nb, N_PAIRS, PAIR, PAIR), lambda g, c: (g, 0, 0, 0))
    body = functools.partial(_dn_scan_body, rows=rows, nb=nb, n_steps=n_steps)
    o, s_new = pl.pallas_call(
        body,
        grid=(n_streams // nb, n_steps),
        in_specs=[blk(), blk(), blk(), blk(), blk(),
                  pl.BlockSpec((nb, 1, 1, WIDTH), lambda g, c: (g, c, 0, 0)), tok(), st(),
                  _const_spec((1, WIDTH)), _const_spec((WIDTH, WIDTH))],
        out_specs=[tok(), st()],
        out_shape=[jax.ShapeDtypeStruct((n_streams, seq, WIDTH), F32),
                   jax.ShapeDtypeStruct((n_streams, N_PAIRS, PAIR, PAIR), F32)],
        scratch_shapes=[pltpu.VMEM((nb, N_PAIRS, PAIR, PAIR), F32), pltpu.VMEM((nb, L, WIDTH), F32)],
        compiler_params=_params(("parallel", "arbitrary")),
        name="dn_scan",
    )(as3(u), as3(w), as3(qe), as3(ks), as3(qk), egl.reshape(n_streams, n_steps, 1, WIDTH),
      z.reshape(n_streams, seq, WIDTH), s0_bd, wts["gout"], wts["bd_w"])
    return o.reshape(n_streams * seq, WIDTH), s_new


def _sb_blocks(probs, upper, valid):
    zs = [_dot_nt(q, k) for q, k, _, _ in probs]
    keeps, betas = [], []
    for z in zs:
        nz = -z
        log_keep = jnp.minimum(nz, 0.0) - jnp.log(1.0 + jnp.exp(jnp.minimum(z, nz)))
        betas.append(z + log_keep)
        keeps.append(log_keep if valid is None else jnp.where(valid, log_keep, 0.0))
    rests = [_dot(lk.astype(BF16), upper) for lk in keeps]
    out = []
    for (_, _, v, carry), lk, lb, rest in zip(probs, keeps, betas, rests):
        a = jnp.exp(lb + (rest + carry))
        if valid is not None:
            a = jnp.where(valid, a, 0.0)
        out.append((_dot(a.astype(BF16), v), carry + jnp.sum(lk, axis=1, keepdims=True)))
    return out


def _upper(n):
    r = lax.broadcasted_iota(jnp.int32, (n, n), 0)
    c = lax.broadcasted_iota(jnp.int32, (n, n), 1)
    return (r > c).astype(BF16)


def _sb_prompt_body(q_ref, k_ref, v_ref, o_ref):
    i = pl.program_id(1)
    t = SB_BLOCK
    upper = _upper(t)
    r = lax.broadcasted_iota(jnp.int32, (2 * t, t), 0)
    c = lax.broadcasted_iota(jnp.int32, (2 * t, t), 1)
    strict = c < (r & (t - 1))
    head0 = _lane_head0((t, PAIR))
    zero = jnp.zeros((2 * t, 1), F32)
    for p0 in range(0, N_PAIRS, SB_PAIRS):
        sls = [slice(p * PAIR, (p + 1) * PAIR) for p in range(p0, p0 + SB_PAIRS)]
        qs = [_bdiag(q_ref[:, sl]) for sl in sls]

        def blocks(off, carries, valid):
            return _sb_blocks([(q, k_ref[pl.ds(off, t), sl], v_ref[pl.ds(off, t), sl], c)
                               for q, sl, c in zip(qs, sls, carries)], upper, valid)

        first = blocks(pl.multiple_of(i * t, t), [zero] * SB_PAIRS, strict)

        def step(n, st):
            res = blocks(pl.multiple_of((i - 1 - n) * t, t), [c for _, c in st], None)
            return tuple((acc + pv, c) for (acc, _), (pv, c) in zip(st, res))

        last = lax.fori_loop(0, i, step, tuple(first))
        for sl, (acc, _) in zip(sls, last):
            o_ref[:, sl] = jnp.where(head0, acc[0:t], acc[t:2 * t])


def _sb_prompt(q16, k16, v16, batch, seq):
    t = SB_BLOCK
    nq = seq // t
    q3 = q16.reshape(batch, seq, WIDTH)
    k3 = k16.reshape(batch, seq, WIDTH)
    v3 = v16.reshape(batch, seq, WIDTH)
    full = pl.BlockSpec((None, seq, WIDTH), lambda b, i: (b, 0, 0))
    blk = pl.BlockSpec((None, t, WIDTH), lambda b, i: (b, i, 0))
    out = pl.pallas_call(
        _sb_prompt_body,
        grid=(batch, nq),
        in_specs=[blk, full, full],
        out_specs=blk,
        out_shape=jax.ShapeDtypeStruct((batch, seq, WIDTH), F32),
        compiler_params=_params(("parallel", "arbitrary")),
        name="sb_prompt",
    )(q3, k3, v3)
    return out.reshape(batch * seq, WIDTH)


def _gather_heads(ref, start, n):
    cols = [ref[pl.ds(start * N_HEADS + h, n, stride=N_HEADS), :] for h in range(N_HEADS)]
    return jnp.concatenate(cols, axis=-1).astype(BF16)


def _sb_sample_body(q_ref, kn_ref, vn_ref, kp_ref, vp_ref, o_ref, qs_ref, acc_ref, carry_ref, *, tq, tk, n_steps):
    j = pl.program_id(1)
    t = SB_BLOCK
    m = N_HEADS * tq
    upper = _upper(t)

    @pl.when(j == 0)
    def _():
        q = q_ref[...]
        lane = lax.broadcasted_iota(jnp.int32, (tq, WIDTH), 1)
        for h in range(N_HEADS):
            hm = (lane >= h * HEAD_DIM) & (lane < (h + 1) * HEAD_DIM)
            qs_ref[h * tq:(h + 1) * tq, :] = jnp.where(hm, q, jnp.zeros((), BF16))
        kn = jnp.concatenate([kn_ref[...], jnp.zeros((t - tq, WIDTH), BF16)], axis=0)
        vn = jnp.concatenate([vn_ref[...], jnp.zeros((t - tq, WIDTH), BF16)], axis=0)
        r = lax.broadcasted_iota(jnp.int32, (m, t), 0)
        c = lax.broadcasted_iota(jnp.int32, (m, t), 1)
        valid = c < (r & (tq - 1))
        (pv, carry), = _sb_blocks([(qs_ref[...], kn, vn, jnp.zeros((m, 1), F32))], upper, valid)
        acc_ref[...] = pv
        carry_ref[...] = carry

    qs = qs_ref[...]
    acc = acc_ref[...]
    carry = carry_ref[...]
    for s in reversed(range(tk // t)):
        (pv, carry), = _sb_blocks([(qs, _gather_heads(kp_ref, s * t, t), _gather_heads(vp_ref, s * t, t), carry)],
                                  upper, None)
        acc = acc + pv
    acc_ref[...] = acc
    carry_ref[...] = carry

    @pl.when(j == n_steps - 1)
    def _():
        lane = lax.broadcasted_iota(jnp.int32, (tq, WIDTH), 1)
        out = jnp.zeros((tq, WIDTH), F32)
        for h in range(N_HEADS):
            hm = (lane >= h * HEAD_DIM) & (lane < (h + 1) * HEAD_DIM)
            out = out + jnp.where(hm, acc[h * tq:(h + 1) * tq, :], 0.0)
        o_ref[...] = out


def _sb_sample(q16, k16, v16, past_k, past_v, batch, tq, tk):
    past = past_k.shape[1] // N_HEADS
    n_steps = past // tk
    assert tq & (tq - 1) == 0 and tq <= SB_BLOCK
    new = lambda: pl.BlockSpec((None, tq, WIDTH), lambda b, j: (b, 0, 0))
    old = lambda: pl.BlockSpec((None, tk * N_HEADS, HEAD_DIM), lambda b, j: (b, n_steps - 1 - j, 0))
    m = N_HEADS * tq
    body = functools.partial(_sb_sample_body, tq=tq, tk=tk, n_steps=n_steps)
    out = pl.pallas_call(
        body,
        grid=(batch, n_steps),
        in_specs=[new(), new(), new(), old(), old()],
        out_specs=new(),
        out_shape=jax.ShapeDtypeStruct((batch, tq, WIDTH), F32),
        scratch_shapes=[pltpu.VMEM((m, WIDTH), BF16), pltpu.VMEM((m, WIDTH), F32), pltpu.VMEM((m, 1), F32)],
        compiler_params=_params(("parallel", "arbitrary")),
        name="sb_sample",
    )(q16.reshape(batch, tq, WIDTH), k16.reshape(batch, tq, WIDTH), v16.reshape(batch, tq, WIDTH),
      past_k, past_v)
    return out.reshape(batch * tq, WIDTH)


def _post_body(x_ref, oa_ref, ob_ref, p_ref, wo_ref, gm_ref, wu_ref, wd_ref, gp_ref, wg_ref, wp_ref, y_ref):
    def rms(h, g):
        ms = jnp.mean(h * h, axis=-1, keepdims=True)
        return (h * lax.rsqrt(ms + NORM_EPS) * g).astype(BF16)

    y_ref[...] = x_ref[...] + _dot(oa_ref[...].astype(BF16), wo_ref[0:WIDTH, :]) \
        + _dot(ob_ref[...].astype(BF16), wo_ref[WIDTH:2 * WIDTH, :])
    h = y_ref[...]
    u = rms(h, gm_ref[...])
    ff = D_MODEL
    for j in range(D_FF // ff):
        up = jnp.maximum(_dot(u, wu_ref[:, j * ff:(j + 1) * ff]), 0.0)
        h = h + _dot((up * up).astype(BF16), wd_ref[j * ff:(j + 1) * ff, :])
    y_ref[...] = h
    h = y_ref[...]
    gate = _sigmoid(_dot(rms(h, gp_ref[...]), wg_ref[...]))
    y_ref[...] = h + gate * _dot(p_ref[...].astype(BF16), wp_ref[...])


def _post(x2d, oa, ob, p2d, w_out, g_mlp, w_up, w_down, g_ple, w_gate, w_proj, tm):
    n = x2d.shape[0]
    row = lambda w: pl.BlockSpec((tm, w), lambda i: (i, 0))
    cs = lambda a: pl.BlockSpec(a.shape, lambda i: (0, 0), pipeline_mode=pl.Buffered(1))
    return pl.pallas_call(
        _post_body,
        grid=(n // tm,),
        in_specs=[row(D_MODEL), row(WIDTH), row(WIDTH), row(PLE_DIM), cs(w_out), cs(g_mlp), cs(w_up), cs(w_down),
                  cs(g_ple), cs(w_gate), cs(w_proj)],
        out_specs=row(D_MODEL),
        out_shape=jax.ShapeDtypeStruct((n, D_MODEL), F32),
        compiler_params=_params(("parallel",)),
        name="post",
    )(x2d, oa, ob, p2d, w_out, g_mlp, w_up, w_down, g_ple, w_gate, w_proj)


def _pair_blockdiag(s):
    b = s.shape[0]
    s = s.reshape(b, N_PAIRS, 2, HEAD_DIM, HEAD_DIM)
    zero = jnp.zeros_like(s[:, :, 0])
    top = jnp.concatenate([s[:, :, 0], zero], axis=-1)
    bot = jnp.concatenate([zero, s[:, :, 1]], axis=-1)
    return jnp.concatenate([top, bot], axis=-2)


def _pair_unblock(s):
    b = s.shape[0]
    h0 = s[:, :, :HEAD_DIM, :HEAD_DIM]
    h1 = s[:, :, HEAD_DIM:, HEAD_DIM:]
    return jnp.stack([h0, h1], axis=2).reshape(b, N_HEADS, HEAD_DIM, HEAD_DIM)


def _expand_matrix():
    lane = jnp.arange(2 * WIDTH)
    src = (lane // WIDTH) * N_HEADS + (lane % WIDTH) // HEAD_DIM
    return (jnp.arange(GATE_PAD)[:, None] == src[None, :]).astype(BF16)


def _group(x, p, conv_state, s0, past_k, past_v, wts, tm, sb_tk):
    batch, seq, _ = x.shape
    n = batch * seq
    x2d = x.reshape(n, D_MODEL)
    qkv, z, ab, q16, k, v, k16, v16 = _inproj(x2d, wts["g_mix"], wts["w_main"], wts["w_ab"], wts["bd_w"],
                                              wts["gq"], wts["gk"], tm)
    prep = _dn_prep(qkv, ab, conv_state, wts, batch, seq)
    o_a, s_new = _dn_scan(prep, z, _pair_blockdiag(s0), wts, batch, seq)
    new_conv = qkv.reshape(batch, seq, CONV_DIM)[:, seq - (CONV_K - 1):, :]
    if past_k is None:
        o_b = _sb_prompt(q16, k16, v16, batch, seq)
    else:
        o_b = _sb_sample(q16, k16, v16, past_k, past_v, batch, seq, sb_tk)
    y = _post(x2d, o_a, o_b, p.reshape(n, PLE_DIM), wts["w_out"], wts["g_mlp"], wts["w_up"], wts["w_down"],
              wts["g_ple"], wts["w_gate"], wts["w_proj"], tm)
    return (y.reshape(batch, seq, D_MODEL), new_conv, _pair_unblock(s_new),
            k.reshape(batch, seq, N_HEADS, HEAD_DIM), v.reshape(batch, seq, N_HEADS, HEAD_DIM))


def _layer_weights(i, g_mix, w_in, conv_w, a_log, dt_bias, g_out_dn, g_q_sb, g_k_sb, w_out, g_mlp, w_up, w_down,
                   g_ple, w_ple_gate, w_ple_proj):
    w = w_in[i]
    gate0 = CONV_DIM + WIDTH
    sb0 = gate0 + 2 * N_HEADS
    pad_row = lambda a: jnp.pad(a.astype(F32), (0, GATE_PAD - a.shape[0])).reshape(1, GATE_PAD)
    return {
        "g_mix": g_mix[i].reshape(1, D_MODEL),
        "w_main": jnp.concatenate([w[:, :gate0], w[:, sb0:]], axis=1).astype(BF16),
        "w_ab": jnp.pad(w[:, gate0:sb0], ((0, 0), (0, GATE_PAD - 2 * N_HEADS))).astype(BF16),
        "bd_w": _head_blockdiag(WIDTH, HEAD_DIM),
        "expand": _expand_matrix(),
        "gq": jnp.tile(g_q_sb[i], N_HEADS).reshape(1, WIDTH),
        "gk": jnp.tile(g_k_sb[i], N_HEADS).reshape(1, WIDTH),
        "conv_w": conv_w[i],
        "alog": pad_row(a_log[i]),
        "dtb": pad_row(dt_bias[i]),
        "gout": jnp.tile(g_out_dn[i], N_HEADS).reshape(1, WIDTH),
        "w_out": w_out[i].astype(BF16),
        "g_mlp": g_mlp[i].reshape(1, D_MODEL),
        "w_up": w_up[i].astype(BF16),
        "w_down": w_down[i].astype(BF16),
        "g_ple": g_ple[i].reshape(1, D_MODEL),
        "w_gate": w_ple_gate[i].astype(BF16),
        "w_proj": w_ple_proj[i].astype(BF16),
    }


def kernel(x_prompt, x_sample, cache_conv, state_delta, cache_k, cache_v, p_prompt, p_sample, g_mix, w_in, conv_w, a_log, dt_bias, g_out_dn, g_q_sb, g_k_sb, w_out, g_mlp, w_up, w_down, g_ple, w_ple_gate, w_ple_proj):
    depth = w_in.shape[0]
    bp = x_prompt.shape[0]
    y_p, y_s = x_prompt, x_sample
    outs_p, outs_s = [], []
    for i in range(depth):
        wts = _layer_weights(i, g_mix, w_in, conv_w, a_log, dt_bias, g_out_dn, g_q_sb, g_k_sb, w_out, g_mlp,
                             w_up, w_down, g_ple, w_ple_gate, w_ple_proj)
        y_p, *rest = _group(y_p, p_prompt[i], jnp.zeros((bp, CONV_K - 1, CONV_DIM), F32),
                            jnp.zeros((bp, N_HEADS, HEAD_DIM, HEAD_DIM), F32), None, None, wts,
                            tm=512, sb_tk=None)
        outs_p.append(rest)
        dec_b, past_len = cache_k.shape[1], cache_k.shape[2]
        y_s, *rest = _group(y_s, p_sample[i], cache_conv[i], state_delta[i],
                            cache_k[i].reshape(dec_b, past_len * N_HEADS, HEAD_DIM),
                            cache_v[i].reshape(dec_b, past_len * N_HEADS, HEAD_DIM),
                            wts, tm=512, sb_tk=min(1024, past_len))
        outs_s.append(rest)
    stack = lambda outs, j: jnp.stack([o[j] for o in outs])
    return (y_p, y_s,
            stack(outs_p, 0), stack(outs_p, 1), stack(outs_p, 2), stack(outs_p, 3),
            stack(outs_s, 0), stack(outs_s, 1), stack(outs_s, 2), stack(outs_s, 3))
```

```python
import functools
import math

import jax
import jax.numpy as jnp
from jax import lax
from jax.experimental import pallas as pl
from jax.experimental.pallas import tpu as pltpu

F32 = jnp.float32
BF16 = jnp.bfloat16

D_MODEL = 1024
HEAD_DIM = 64
N_HEADS = 8
WIDTH = N_HEADS * HEAD_DIM
N_PAIRS = N_HEADS // 2
PAIR = 2 * HEAD_DIM
CONV_K = 4
CONV_DIM = 3 * WIDTH
CHUNK = 64
D_FF = 4 * D_MODEL
PLE_DIM = 256
SB_SCALE = HEAD_DIM ** -0.5
NORM_EPS = 1e-6
GATE_PAD = 128
SB_BLOCK = 256
SB_PAIRS = 4
VMEM_LIMIT = 56 * 1024 * 1024


def _dot(a, b):
    return lax.dot_general(a, b, (((1,), (0,)), ((), ())), preferred_element_type=F32)


def _dot_nt(a, b):
    return lax.dot_general(a, b, (((1,), (1,)), ((), ())), preferred_element_type=F32)


def _dot_tn(a, b):
    return lax.dot_general(a, b, (((0,), (0,)), ((), ())), preferred_element_type=F32)


def _split(x, n):
    parts = []
    r = x
    for i in range(n):
        p = r.astype(BF16)
        parts.append(p)
        if i + 1 < n:
            r = r - p.astype(F32)
    return parts


def _mm(dot, a, b, na, nb):
    ap = [a] if a.dtype == BF16 else _split(a, na)
    bp = [b] if b.dtype == BF16 else _split(b, nb)
    out = None
    for i, x in enumerate(ap):
        for j, y in enumerate(bp):
            if i + j >= max(len(ap), len(bp)):
                continue
            t = dot(x, y)
            out = t if out is None else out + t
    return out


def _softplus(x):
    return jnp.maximum(x, 0.0) + jnp.log1p(jnp.exp(-jnp.abs(x)))


def _sigmoid(x):
    return 1.0 / (1.0 + jnp.exp(-x))


def _silu(x):
    return x * _sigmoid(x)


def _const_spec(shape):
    n = len(shape)
    return pl.BlockSpec(shape, lambda *_: (0,) * n)


def _params(sem):
    return pltpu.CompilerParams(dimension_semantics=sem, vmem_limit_bytes=VMEM_LIMIT)


def _head_blockdiag(n, blk):
    r = jnp.arange(n) // blk
    return (r[:, None] == r[None, :]).astype(BF16)


def _inproj_body(x_ref, g_ref, w_ref, wab_ref, bd_ref, gq_ref, gk_ref,
                 qkv_ref, z_ref, ab_ref, q_ref, k_ref, v_ref, k16_ref, v16_ref, *, seq_minor):
    x = x_ref[...]
    ms = jnp.mean(x * x, axis=-1, keepdims=True)
    u = (x * lax.rsqrt(ms + NORM_EPS) * g_ref[...]).astype(BF16)
    qkv_ref[...] = _dot(u, w_ref[:, 0:CONV_DIM])
    z_ref[...] = _dot(u, w_ref[:, CONV_DIM:CONV_DIM + WIDTH])
    ab_ref[...] = _dot(u, wab_ref[...])
    o = CONV_DIM + WIDTH
    bd = bd_ref[...]

    def headnorm(t, g):
        ss = _mm(_dot, t * t, bd, 2, 1)
        return t * lax.rsqrt(ss * (1.0 / HEAD_DIM) + NORM_EPS) * g

    q = headnorm(_dot(u, w_ref[:, o:o + WIDTH]), gq_ref[...])
    k = headnorm(_dot(u, w_ref[:, o + WIDTH:o + 2 * WIDTH]), gk_ref[...])
    v = _dot(u, w_ref[:, o + 2 * WIDTH:o + 3 * WIDTH])
    q_ref[...] = (q * SB_SCALE).astype(BF16)
    tm = k.shape[0]
    if seq_minor:
        k_ref[...] = k.T
        v_ref[...] = v.T
    else:
        for h in range(N_HEADS):
            k_ref[pl.ds(h, tm, stride=N_HEADS), :] = k[:, h * HEAD_DIM:(h + 1) * HEAD_DIM]
            v_ref[pl.ds(h, tm, stride=N_HEADS), :] = v[:, h * HEAD_DIM:(h + 1) * HEAD_DIM]
    k16_ref[...] = k.astype(BF16)
    v16_ref[...] = v.astype(BF16)


def _inproj(x2d, g_mix, w_main, w_ab, bd, gq, gk, tm, seq):
    n = x2d.shape[0]
    seq_minor = seq % tm == 0
    row = lambda w: pl.BlockSpec((tm, w), lambda i: (i, 0))
    if seq_minor:
        tiles = seq // tm
        kv_spec = pl.BlockSpec((None, WIDTH, tm), lambda i: (i // tiles, 0, i % tiles))
        kv_shape = jax.ShapeDtypeStruct((n // seq, WIDTH, seq), F32)
    else:
        kv_spec = pl.BlockSpec((tm * N_HEADS, HEAD_DIM), lambda i: (i, 0))
        kv_shape = jax.ShapeDtypeStruct((n * N_HEADS, HEAD_DIM), F32)
    outs = [(CONV_DIM, F32), (WIDTH, F32), (GATE_PAD, F32), (WIDTH, BF16), None, None, (WIDTH, BF16), (WIDTH, BF16)]
    return pl.pallas_call(
        functools.partial(_inproj_body, seq_minor=seq_minor),
        grid=(n // tm,),
        in_specs=[row(D_MODEL), _const_spec((1, D_MODEL)), _const_spec(w_main.shape), _const_spec(w_ab.shape),
                  _const_spec(bd.shape), _const_spec((1, WIDTH)), _const_spec((1, WIDTH))],
        out_specs=[kv_spec if o is None else row(o[0]) for o in outs],
        out_shape=[kv_shape if o is None else jax.ShapeDtypeStruct((n, o[0]), o[1]) for o in outs],
        compiler_params=_params(("parallel",)),
        name="inproj",
    )(x2d, g_mix, w_main, w_ab, bd, gq, gk)


DN_CHUNKS = 4
DN_STREAMS = 8
DN_PRECISE_FACTORS = 3


def _lane_head0(shape):
    return lax.broadcasted_iota(jnp.int32, shape, len(shape) - 1) % PAIR < HEAD_DIM


def _bdiag(x):
    m0 = _lane_head0(x.shape)
    zero = jnp.zeros((), x.dtype)
    return jnp.concatenate([jnp.where(m0, x, zero), jnp.where(m0, zero, x)], axis=0)


def _dn_prep_body(qkv_ref, halo_ref, ab_ref, cs_ref, cw_ref, alog_ref, dtb_ref, bdw_ref, expand_ref,
                  u_ref, w_ref, qe_ref, ks_ref, qk_ref, egl_ref,
                  xp_ref, y_ref, gb_ref, *, seg_rows, n_seg, blocks_per_stream):
    L = CHUNK
    pad = max(seg_rows, L)
    R = n_seg * pad
    nc = R // L

    for s in range(n_seg):
        if blocks_per_stream:
            xp_ref[s, 5:8, :] = halo_ref[5:8, :]

            @pl.when(pl.program_id(0) % blocks_per_stream == 0)
            def _():
                xp_ref[s, 5:8, :] = cs_ref[...]
        else:
            xp_ref[s, 5:8, :] = cs_ref[s]
        xp_ref[s, 8:8 + seg_rows, :] = qkv_ref[s * seg_rows:(s + 1) * seg_rows, :]
        y = xp_ref[s, 5:5 + seg_rows, :] * cw_ref[0:1, :]
        for i in range(1, CONV_K):
            y = y + xp_ref[s, 5 + i:5 + i + seg_rows, :] * cw_ref[i:i + 1, :]
        y_ref[s * pad:s * pad + seg_rows, :] = _silu(y)
        ab = ab_ref[s * seg_rows:(s + 1) * seg_rows, :]
        col = lax.broadcasted_iota(jnp.int32, ab.shape, 1)
        g = -jnp.exp(alog_ref[...]) * _softplus(ab + dtb_ref[...])
        gb_ref[s * pad:s * pad + seg_rows, :] = jnp.where(col < N_HEADS, g, _sigmoid(ab))
        if seg_rows < pad:
            y_ref[s * pad + seg_rows:(s + 1) * pad, :] = jnp.zeros((pad - seg_rows, CONV_DIM), F32)
            gb_ref[s * pad + seg_rows:(s + 1) * pad, :] = jnp.zeros((pad - seg_rows, GATE_PAD), F32)

    bdw = bdw_ref[...]
    q = y_ref[:, 0:WIDTH]
    k = y_ref[:, WIDTH:2 * WIDTH]
    q = q * lax.rsqrt(_mm(_dot, q * q, bdw, 2, 1) + NORM_EPS) * SB_SCALE
    k = k * lax.rsqrt(_mm(_dot, k * k, bdw, 2, 1) + NORM_EPS)

    rr = lax.broadcasted_iota(jnp.int32, (R, R), 0)
    cc = lax.broadcasted_iota(jnp.int32, (R, R), 1)
    same = (rr // L) == (cc // L)
    blk_lower = (same & (rr >= cc)).astype(BF16)
    blk_ones = same.astype(BF16)
    gb = gb_ref[...]
    colg = lax.broadcasted_iota(jnp.int32, (R, GATE_PAD), 1)
    gcum = jnp.where(colg < N_HEADS, _mm(_dot, blk_lower, gb, 1, 3), gb)
    ex = _mm(_dot, gcum, expand_ref[...], 3, 1)
    gc_all = ex[:, 0:WIDTH]
    beta_all = ex[:, WIDTH:2 * WIDTH]
    r512 = lax.broadcasted_iota(jnp.int32, (R, WIDTH), 0)
    l512 = lax.broadcasted_iota(jnp.int32, (R, WIDTH), 1)
    eye = (r512 % L) == (l512 % HEAD_DIM)
    gct_all = _mm(_dot, blk_ones, jnp.where(eye, gc_all, 0.0), 1, 2)

    row = lax.broadcasted_iota(jnp.int32, (L, PAIR), 0)
    jcol = lax.broadcasted_iota(jnp.int32, (L, PAIR), 1) % HEAD_DIM
    causal = row >= jcol
    strict = row > jcol

    chains = [(slice(c * L, (c + 1) * L), slice(p * PAIR, (p + 1) * PAIR)) for c in range(nc) for p in range(N_PAIRS)]
    for c in range(nc):
        egl_ref[c] = jnp.exp(gc_all[(c + 1) * L - 1:(c + 1) * L, :])
    a_list, x_list = [], []
    for rs, sl in chains:
        qc = q[rs, sl]
        kc = k[rs, sl]
        vc = y_ref[rs, 2 * WIDTH + sl.start:2 * WIDTH + sl.stop]
        gc = gc_all[rs, sl]
        beta = beta_all[rs, sl]
        eg = jnp.exp(gc)
        decay = jnp.where(causal, jnp.exp(jnp.where(causal, gc - gct_all[rs, sl], 0.0)), 0.0)
        kb = kc * beta
        kst = _bdiag(kc.astype(BF16))
        aq = _dot_nt(jnp.concatenate([kb, qc], axis=0).astype(BF16), kst)
        a_list.append(jnp.where(strict, aq[0:L] * decay, 0.0))
        x_list.append(jnp.concatenate([vc * beta, kb * eg], axis=1))
        qk_ref[rs, sl] = jnp.where(causal, aq[L:2 * L] * decay, 0.0).astype(BF16)
        qe_ref[rs, sl] = (qc * eg).astype(BF16)
        ks_ref[rs, sl] = (kc * jnp.exp(gc[L - 1:L, :] - gc)).astype(BF16)
    for t in range(6):
        np_ = 2 if t < DN_PRECISE_FACTORS else 1
        for n in range(len(chains)):
            ak, x = a_list[n], x_list[n]
            if np_ == 1:
                ak, x = ak.astype(BF16), x.astype(BF16)
            if t < 5:
                r = _mm(_dot, ak, jnp.concatenate([_bdiag(ak), _bdiag(x)], axis=1), np_, np_)
                a_list[n], ax = r[:, 0:PAIR], r[:, PAIR:]
            else:
                ax = _mm(_dot, ak, _bdiag(x), np_, np_)
            x_list[n] = x_list[n] - ax if t == 0 else x_list[n] + ax
    for (rs, sl), x in zip(chains, x_list):
        u_ref[rs, sl] = x[:, 0:PAIR]
        w_ref[rs, sl] = x[:, PAIR:].astype(BF16)


def _dn_prep(qkv, ab, conv_state, wts, n_streams, seq):
    L = CHUNK
    if seq >= L:
        seg_rows, n_seg = DN_CHUNKS * L, 1
        blocks_per_stream = seq // seg_rows
        cs_spec = pl.BlockSpec((None, CONV_K - 1, CONV_DIM), lambda i: (i // blocks_per_stream, 0, 0))
    else:
        seg_rows, n_seg, blocks_per_stream = seq, DN_CHUNKS, 0
        cs_spec = pl.BlockSpec((n_seg, CONV_K - 1, CONV_DIM), lambda i: (i, 0, 0))
    rows_in = n_seg * seg_rows
    R = n_seg * max(seg_rows, L)
    n_blocks = n_streams * seq // rows_in
    tok = lambda w: pl.BlockSpec((rows_in, w), lambda i: (i, 0))
    halo = pl.BlockSpec((8, CONV_DIM), lambda i: (jnp.maximum(i * (rows_in // 8) - 1, 0), 0))
    out = lambda: pl.BlockSpec((R, WIDTH), lambda i: (i, 0))
    n_rows = n_blocks * R
    body = functools.partial(_dn_prep_body, seg_rows=seg_rows, n_seg=n_seg, blocks_per_stream=blocks_per_stream)
    return pl.pallas_call(
        body,
        grid=(n_blocks,),
        in_specs=[tok(CONV_DIM), halo, tok(GATE_PAD), cs_spec,
                  _const_spec((CONV_K, CONV_DIM)), _const_spec((1, GATE_PAD)), _const_spec((1, GATE_PAD)),
                  _const_spec((WIDTH, WIDTH)), _const_spec((GATE_PAD, 2 * WIDTH))],
        out_specs=[out(), out(), out(), out(), out(), pl.BlockSpec((R // L, 1, WIDTH), lambda i: (i, 0, 0))],
        out_shape=[jax.ShapeDtypeStruct((n_rows, WIDTH), F32)] +
                  [jax.ShapeDtypeStruct((n_rows, WIDTH), BF16)] * 4 +
                  [jax.ShapeDtypeStruct((n_rows // L, 1, WIDTH), F32)],
        scratch_shapes=[pltpu.VMEM((n_seg, seg_rows + 8, CONV_DIM), F32), pltpu.VMEM((R, CONV_DIM), F32),
                        pltpu.VMEM((R, GATE_PAD), F32)],
        compiler_params=_params(("parallel",)),
        name="dn_prep",
    )(qkv, qkv, ab, conv_state, wts["conv_w"], wts["alog"], wts["dtb"], wts["bd_w"], wts["expand"])


def _dn_scan_body(u_ref, w_ref, qe_ref, ks_ref, qk_ref, egl_ref, z_ref, s0_ref, gout_ref, bdw_ref,
                  o_ref, s_out_ref, s_ref, oraw_ref, *, rows, nb, n_steps):
    c = pl.program_id(1)
    L = CHUNK

    @pl.when(c == 0)
    def _():
        s_ref[...] = s0_ref[...]

    r = lax.broadcasted_iota(jnp.int32, (PAIR, PAIR), 0)
    cl = lax.broadcasted_iota(jnp.int32, (PAIR, PAIR), 1)
    same_head = (r // HEAD_DIM) == (cl // HEAD_DIM)
    chains = [(b, p, slice(p * PAIR, (p + 1) * PAIR)) for b in range(nb) for p in range(N_PAIRS)]
    group = 4 * N_PAIRS
    for g0 in range(0, len(chains), group):
        grp = chains[g0:g0 + group]
        ws = [_dot(jnp.concatenate([w_ref[b, :, sl], qe_ref[b, :, sl]], axis=0), s_ref[b, p].astype(BF16))
              for b, p, sl in grp]
        v_new = [(u_ref[b, :, sl] - r[0:L]).astype(BF16) for (b, p, sl), r in zip(grp, ws)]
        for (b, p, sl), r, v in zip(grp, ws, v_new):
            oraw_ref[b, :, sl] = r[L:2 * L] + _dot(qk_ref[b, :, sl], _bdiag(v))
        for (b, p, sl), v in zip(grp, v_new):
            s_ref[b, p] = (s_ref[b, p] * egl_ref[b, 0, :, sl]
                           + jnp.where(same_head, _dot_tn(ks_ref[b, :, sl], v), 0.0))

    o = oraw_ref[...].reshape(nb * L, WIDTH)
    o = o * lax.rsqrt(_mm(_dot, o * o, bdw_ref[...], 2, 1) * (1.0 / HEAD_DIM) + NORM_EPS) * gout_ref[...]
    for b in range(nb):
        o_ref[b] = o[b * L:b * L + rows] * _silu(z_ref[b])

    @pl.when(c == n_steps - 1)
    def _():
        s_out_ref[...] = s_ref[...]


def _dn_scan(prep, z, s0_bd, wts, n_streams, seq):
    L = CHUNK
    rows = min(L, seq)
    n_steps = seq // rows
    nb = DN_STREAMS
    u, w, qe, ks, qk, egl = prep
    as3 = lambda a: a.reshape(n_streams, n_steps * L, WIDTH)
    blk = lambda: pl.BlockSpec((nb, L, WIDTH), lambda g, c: (g, c, 0))
    tok = lambda: pl.BlockSpec((nb, rows, WIDTH), lambda g, c: (g, c, 0))
    st = lambda: pl.BlockSpec((nb, N_PAIRS, PAIR, PAIR), lambda g, c: (g, 0, 0, 0))
    body = functools.partial(_dn_scan_body, rows=rows, nb=nb, n_steps=n_steps)
    o, s_new = pl.pallas_call(
        body,
        grid=(n_streams // nb, n_steps),
        in_specs=[blk(), blk(), blk(), blk(), blk(),
                  pl.BlockSpec((nb, 1, 1, WIDTH), lambda g, c: (g, c, 0, 0)), tok(), st(),
                  _const_spec((1, WIDTH)), _const_spec((WIDTH, WIDTH))],
        out_specs=[tok(), st()],
        out_shape=[jax.ShapeDtypeStruct((n_streams, seq, WIDTH), F32),
                   jax.ShapeDtypeStruct((n_streams, N_PAIRS, PAIR, PAIR), F32)],
        scratch_shapes=[pltpu.VMEM((nb, N_PAIRS, PAIR, PAIR), F32), pltpu.VMEM((nb, L, WIDTH), F32)],
        compiler_params=_params(("parallel", "arbitrary")),
        name="dn_scan",
    )(as3(u), as3(w), as3(qe), as3(ks), as3(qk), egl.reshape(n_streams, n_steps, 1, WIDTH),
      z.reshape(n_streams, seq, WIDTH), s0_bd, wts["gout"], wts["bd_w"])
    return o.reshape(n_streams * seq, WIDTH), s_new


def _sb_blocks(probs, upper, valid, keys_minor=False):
    zs = [(_dot if keys_minor else _dot_nt)(q, k) for q, k, _, _ in probs]
    keeps, betas = [], []
    for z in zs:
        nz = -z
        log_keep = jnp.minimum(nz, 0.0) - jnp.log(1.0 + jnp.exp(jnp.minimum(z, nz)))
        betas.append(z + log_keep)
        keeps.append(log_keep if valid is None else jnp.where(valid, log_keep, 0.0))
    rests = [_dot(lk.astype(BF16), upper) for lk in keeps]
    out = []
    for (_, _, v, carry), lk, lb, rest in zip(probs, keeps, betas, rests):
        a = jnp.exp(lb + (rest + carry))
        if valid is not None:
            a = jnp.where(valid, a, 0.0)
        pv = (_dot_nt if keys_minor else _dot)(a.astype(BF16), v)
        out.append((pv, carry + jnp.sum(lk, axis=1, keepdims=True)))
    return out


def _upper(n):
    r = lax.broadcasted_iota(jnp.int32, (n, n), 0)
    c = lax.broadcasted_iota(jnp.int32, (n, n), 1)
    return (r > c).astype(BF16)


def _sb_prompt_body(q_ref, k_ref, v_ref, o_ref):
    i = pl.program_id(1)
    t = SB_BLOCK
    upper = _upper(t)
    r = lax.broadcasted_iota(jnp.int32, (2 * t, t), 0)
    c = lax.broadcasted_iota(jnp.int32, (2 * t, t), 1)
    strict = c < (r & (t - 1))
    head0 = _lane_head0((t, PAIR))
    zero = jnp.zeros((2 * t, 1), F32)
    for p0 in range(0, N_PAIRS, SB_PAIRS):
        sls = [slice(p * PAIR, (p + 1) * PAIR) for p in range(p0, p0 + SB_PAIRS)]
        qs = [_bdiag(q_ref[:, sl]) for sl in sls]

        def blocks(off, carries, valid):
            return _sb_blocks([(q, k_ref[pl.ds(off, t), sl], v_ref[pl.ds(off, t), sl], c)
                               for q, sl, c in zip(qs, sls, carries)], upper, valid)

        first = blocks(pl.multiple_of(i * t, t), [zero] * SB_PAIRS, strict)

        def step(n, st):
            res = blocks(pl.multiple_of((i - 1 - n) * t, t), [c for _, c in st], None)
            return tuple((acc + pv, c) for (acc, _), (pv, c) in zip(st, res))

        last = lax.fori_loop(0, i, step, tuple(first))
        for sl, (acc, _) in zip(sls, last):
            o_ref[:, sl] = jnp.where(head0, acc[0:t], acc[t:2 * t])


def _sb_prompt(q16, k16, v16, batch, seq):
    t = SB_BLOCK
    nq = seq // t
    q3 = q16.reshape(batch, seq, WIDTH)
    k3 = k16.reshape(batch, seq, WIDTH)
    v3 = v16.reshape(batch, seq, WIDTH)
    full = pl.BlockSpec((None, seq, WIDTH), lambda b, i: (b, 0, 0))
    blk = pl.BlockSpec((None, t, WIDTH), lambda b, i: (b, i, 0))
    out = pl.pallas_call(
        _sb_prompt_body,
        grid=(batch, nq),
        in_specs=[blk, full, full],
        out_specs=blk,
        out_shape=jax.ShapeDtypeStruct((batch, seq, WIDTH), F32),
        compiler_params=_params(("parallel", "arbitrary")),
        name="sb_prompt",
    )(q3, k3, v3)
    return out.reshape(batch * seq, WIDTH)


def _sb_sample_body(q_ref, kn_ref, vn_ref, kp_ref, vp_ref, o_ref, qs_ref, acc_ref, carry_ref, *, tq, tk, n_steps):
    j = pl.program_id(1)
    t = SB_BLOCK
    m = N_HEADS * tq
    upper = _upper(t)

    @pl.when(j == 0)
    def _():
        q = q_ref[...]
        lane = lax.broadcasted_iota(jnp.int32, (tq, WIDTH), 1)
        for h in range(N_HEADS):
            hm = (lane >= h * HEAD_DIM) & (lane < (h + 1) * HEAD_DIM)
            qs_ref[h * tq:(h + 1) * tq, :] = jnp.where(hm, q, jnp.zeros((), BF16))
        kn = jnp.concatenate([kn_ref[...], jnp.zeros((t - tq, WIDTH), BF16)], axis=0)
        vn = jnp.concatenate([vn_ref[...], jnp.zeros((t - tq, WIDTH), BF16)], axis=0)
        r = lax.broadcasted_iota(jnp.int32, (m, t), 0)
        c = lax.broadcasted_iota(jnp.int32, (m, t), 1)
        valid = c < (r & (tq - 1))
        (pv, carry), = _sb_blocks([(qs_ref[...], kn, vn, jnp.zeros((m, 1), F32))], upper, valid)
        acc_ref[...] = pv
        carry_ref[...] = carry

    qs = qs_ref[...]
    acc = acc_ref[...]
    carry = carry_ref[...]
    for s in reversed(range(tk // t)):
        (pv, carry), = _sb_blocks([(qs, kp_ref[:, s * t:(s + 1) * t].astype(BF16),
                                     vp_ref[:, s * t:(s + 1) * t].astype(BF16), carry)], upper, None, keys_minor=True)
        acc = acc + pv
    acc_ref[...] = acc
    carry_ref[...] = carry

    @pl.when(j == n_steps - 1)
    def _():
        lane = lax.broadcasted_iota(jnp.int32, (tq, WIDTH), 1)
        out = jnp.zeros((tq, WIDTH), F32)
        for h in range(N_HEADS):
            hm = (lane >= h * HEAD_DIM) & (lane < (h + 1) * HEAD_DIM)
            out = out + jnp.where(hm, acc[h * tq:(h + 1) * tq, :], 0.0)
        o_ref[...] = out


def _sb_sample(q16, k16, v16, past_k, past_v, batch, tq, tk):
    past = past_k.shape[2]
    n_steps = past // tk
    assert tq & (tq - 1) == 0 and tq <= SB_BLOCK
    new = lambda: pl.BlockSpec((None, tq, WIDTH), lambda b, j: (b, 0, 0))
    old = lambda: pl.BlockSpec((None, WIDTH, tk), lambda b, j: (b, 0, n_steps - 1 - j))
    m = N_HEADS * tq
    body = functools.partial(_sb_sample_body, tq=tq, tk=tk, n_steps=n_steps)
    out = pl.pallas_call(
        body,
        grid=(batch, n_steps),
        in_specs=[new(), new(), new(), old(), old()],
        out_specs=new(),
        out_shape=jax.ShapeDtypeStruct((batch, tq, WIDTH), F32),
        scratch_shapes=[pltpu.VMEM((m, WIDTH), BF16), pltpu.VMEM((m, WIDTH), F32), pltpu.VMEM((m, 1), F32)],
        compiler_params=_params(("parallel", "arbitrary")),
        name="sb_sample",
    )(q16.reshape(batch, tq, WIDTH), k16.reshape(batch, tq, WIDTH), v16.reshape(batch, tq, WIDTH),
      past_k, past_v)
    return out.reshape(batch * tq, WIDTH)


def _post_body(x_ref, oa_ref, ob_ref, p_ref, wo_ref, gm_ref, wu_ref, wd_ref, gp_ref, wg_ref, wp_ref, y_ref):
    def rms(h, g):
        ms = jnp.mean(h * h, axis=-1, keepdims=True)
        return (h * lax.rsqrt(ms + NORM_EPS) * g).astype(BF16)

    y_ref[...] = x_ref[...] + _dot(oa_ref[...].astype(BF16), wo_ref[0:WIDTH, :]) \
        + _dot(ob_ref[...].astype(BF16), wo_ref[WIDTH:2 * WIDTH, :])
    h = y_ref[...]
    u = rms(h, gm_ref[...])
    ff = D_MODEL
    for j in range(D_FF // ff):
        up = jnp.maximum(_dot(u, wu_ref[:, j * ff:(j + 1) * ff]), 0.0)
        h = h + _dot((up * up).astype(BF16), wd_ref[j * ff:(j + 1) * ff, :])
    y_ref[...] = h
    h = y_ref[...]
    gate = _sigmoid(_dot(rms(h, gp_ref[...]), wg_ref[...]))
    y_ref[...] = h + gate * _dot(p_ref[...].astype(BF16), wp_ref[...])


def _post(x2d, oa, ob, p2d, w_out, g_mlp, w_up, w_down, g_ple, w_gate, w_proj, tm):
    n = x2d.shape[0]
    row = lambda w: pl.BlockSpec((tm, w), lambda i: (i, 0))
    cs = lambda a: pl.BlockSpec(a.shape, lambda i: (0, 0), pipeline_mode=pl.Buffered(1))
    return pl.pallas_call(
        _post_body,
        grid=(n // tm,),
        in_specs=[row(D_MODEL), row(WIDTH), row(WIDTH), row(PLE_DIM), cs(w_out), cs(g_mlp), cs(w_up), cs(w_down),
                  cs(g_ple), cs(w_gate), cs(w_proj)],
        out_specs=row(D_MODEL),
        out_shape=jax.ShapeDtypeStruct((n, D_MODEL), F32),
        compiler_params=_params(("parallel",)),
        name="post",
    )(x2d, oa, ob, p2d, w_out, g_mlp, w_up, w_down, g_ple, w_gate, w_proj)


def _pair_blockdiag(s):
    b = s.shape[0]
    s = s.reshape(b, N_PAIRS, 2, HEAD_DIM, HEAD_DIM)
    zero = jnp.zeros_like(s[:, :, 0])
    top = jnp.concatenate([s[:, :, 0], zero], axis=-1)
    bot = jnp.concatenate([zero, s[:, :, 1]], axis=-1)
    return jnp.concatenate([top, bot], axis=-2)


def _pair_unblock(s):
    b = s.shape[0]
    h0 = s[:, :, :HEAD_DIM, :HEAD_DIM]
    h1 = s[:, :, HEAD_DIM:, HEAD_DIM:]
    return jnp.stack([h0, h1], axis=2).reshape(b, N_HEADS, HEAD_DIM, HEAD_DIM)


def _expand_matrix():
    lane = jnp.arange(2 * WIDTH)
    src = (lane // WIDTH) * N_HEADS + (lane % WIDTH) // HEAD_DIM
    return (jnp.arange(GATE_PAD)[:, None] == src[None, :]).astype(BF16)


def _seq_minor(a):
    b, sq, h, d = a.shape
    return a.transpose(0, 2, 3, 1).reshape(b, h * d, sq)


def _group(x, p, conv_state, s0, past_k, past_v, wts, tm, sb_tk):
    batch, seq, _ = x.shape
    n = batch * seq
    x2d = x.reshape(n, D_MODEL)
    qkv, z, ab, q16, k, v, k16, v16 = _inproj(x2d, wts["g_mix"], wts["w_main"], wts["w_ab"], wts["bd_w"],
                                              wts["gq"], wts["gk"], tm, seq)
    prep = _dn_prep(qkv, ab, conv_state, wts, batch, seq)
    o_a, s_new = _dn_scan(prep, z, _pair_blockdiag(s0), wts, batch, seq)
    new_conv = qkv.reshape(batch, seq, CONV_DIM)[:, seq - (CONV_K - 1):, :]
    if past_k is None:
        o_b = _sb_prompt(q16, k16, v16, batch, seq)
    else:
        o_b = _sb_sample(q16, k16, v16, past_k, past_v, batch, seq, sb_tk)
    y = _post(x2d, o_a, o_b, p.reshape(n, PLE_DIM), wts["w_out"], wts["g_mlp"], wts["w_up"], wts["w_down"],
              wts["g_ple"], wts["w_gate"], wts["w_proj"], tm)
    if k.ndim == 3:
        heads = lambda a: a.reshape(batch, N_HEADS, HEAD_DIM, seq).transpose(0, 3, 1, 2)
    else:
        heads = lambda a: a.reshape(batch, seq, N_HEADS, HEAD_DIM)
    return y.reshape(batch, seq, D_MODEL), new_conv, _pair_unblock(s_new), heads(k), heads(v)


def _layer_weights(i, g_mix, w_in, conv_w, a_log, dt_bias, g_out_dn, g_q_sb, g_k_sb, w_out, g_mlp, w_up, w_down,
                   g_ple, w_ple_gate, w_ple_proj):
    w = w_in[i]
    gate0 = CONV_DIM + WIDTH
    sb0 = gate0 + 2 * N_HEADS
    pad_row = lambda a: jnp.pad(a.astype(F32), (0, GATE_PAD - a.shape[0])).reshape(1, GATE_PAD)
    return {
        "g_mix": g_mix[i].reshape(1, D_MODEL),
        "w_main": jnp.concatenate([w[:, :gate0], w[:, sb0:]], axis=1).astype(BF16),
        "w_ab": jnp.pad(w[:, gate0:sb0], ((0, 0), (0, GATE_PAD - 2 * N_HEADS))).astype(BF16),
        "bd_w": _head_blockdiag(WIDTH, HEAD_DIM),
        "expand": _expand_matrix(),
        "gq": jnp.tile(g_q_sb[i], N_HEADS).reshape(1, WIDTH),
        "gk": jnp.tile(g_k_sb[i], N_HEADS).reshape(1, WIDTH),
        "conv_w": conv_w[i],
        "alog": pad_row(a_log[i]),
        "dtb": pad_row(dt_bias[i]),
        "gout": jnp.tile(g_out_dn[i], N_HEADS).reshape(1, WIDTH),
        "w_out": w_out[i].astype(BF16),
        "g_mlp": g_mlp[i].reshape(1, D_MODEL),
        "w_up": w_up[i].astype(BF16),
        "w_down": w_down[i].astype(BF16),
        "g_ple": g_ple[i].reshape(1, D_MODEL),
        "w_gate": w_ple_gate[i].astype(BF16),
        "w_proj": w_ple_proj[i].astype(BF16),
    }


def kernel(x_prompt, x_sample, cache_conv, state_delta, cache_k, cache_v, p_prompt, p_sample, g_mix, w_in, conv_w, a_log, dt_bias, g_out_dn, g_q_sb, g_k_sb, w_out, g_mlp, w_up, w_down, g_ple, w_ple_gate, w_ple_proj):
    depth = w_in.shape[0]
    bp = x_prompt.shape[0]
    y_p, y_s = x_prompt, x_sample
    outs_p, outs_s = [], []
    for i in range(depth):
        wts = _layer_weights(i, g_mix, w_in, conv_w, a_log, dt_bias, g_out_dn, g_q_sb, g_k_sb, w_out, g_mlp,
                             w_up, w_down, g_ple, w_ple_gate, w_ple_proj)
        y_p, *rest = _group(y_p, p_prompt[i], jnp.zeros((bp, CONV_K - 1, CONV_DIM), F32),
                            jnp.zeros((bp, N_HEADS, HEAD_DIM, HEAD_DIM), F32), None, None, wts,
                            tm=512, sb_tk=None)
        outs_p.append(rest)
        dec_b, past_len = cache_k.shape[1], cache_k.shape[2]
        y_s, *rest = _group(y_s, p_sample[i], cache_conv[i], state_delta[i],
                            _seq_minor(cache_k[i]), _seq_minor(cache_v[i]),
                            wts, tm=512, sb_tk=min(1024, past_len))
        outs_s.append(rest)
    stack = lambda outs, j: jnp.stack([o[j] for o in outs])
    return (y_p, y_s,
            stack(outs_p, 0), stack(outs_p, 1), stack(outs_p, 2), stack(outs_p, 3),
            stack(outs_s, 0), stack(outs_s, 1), stack(outs_s, 2), stack(outs_s, 3))
```

```python
import functools
import math

import jax
import jax.numpy as jnp
from jax import lax
from jax.experimental import pallas as pl
from jax.experimental.pallas import tpu as pltpu

F32 = jnp.float32
BF16 = jnp.bfloat16

D_MODEL = 1024
HEAD_DIM = 64
N_HEADS = 8
WIDTH = N_HEADS * HEAD_DIM
N_PAIRS = N_HEADS // 2
PAIR = 2 * HEAD_DIM
CONV_K = 4
CONV_DIM = 3 * WIDTH
CHUNK = 64
D_FF = 4 * D_MODEL
PLE_DIM = 256
SB_SCALE = HEAD_DIM ** -0.5
LOG2E = math.log2(math.e)
NORM_EPS = 1e-6
GATE_PAD = 128
SB_BLOCK = 256
SB_PAIRS = 4
VMEM_LIMIT = 56 * 1024 * 1024


def _dot(a, b):
    return lax.dot_general(a, b, (((1,), (0,)), ((), ())), preferred_element_type=F32)


def _dot_nt(a, b):
    return lax.dot_general(a, b, (((1,), (1,)), ((), ())), preferred_element_type=F32)


def _dot_tn(a, b):
    return lax.dot_general(a, b, (((0,), (0,)), ((), ())), preferred_element_type=F32)


def _split(x, n):
    parts = []
    r = x
    for i in range(n):
        p = r.astype(BF16)
        parts.append(p)
        if i + 1 < n:
            r = r - p.astype(F32)
    return parts


def _mm(dot, a, b, na, nb):
    ap = [a] if a.dtype == BF16 else _split(a, na)
    bp = [b] if b.dtype == BF16 else _split(b, nb)
    out = None
    for i, x in enumerate(ap):
        for j, y in enumerate(bp):
            if i + j >= max(len(ap), len(bp)):
                continue
            t = dot(x, y)
            out = t if out is None else out + t
    return out


def _head_sumsq(t, bd):
    return _dot((t * t).astype(BF16), bd)


def _softplus(x):
    return jnp.maximum(x, 0.0) + jnp.log1p(jnp.exp(-jnp.abs(x)))


def _sigmoid(x):
    return 1.0 / (1.0 + jnp.exp(-x))


def _silu(x):
    return x * _sigmoid(x)


def _const_spec(shape):
    n = len(shape)
    return pl.BlockSpec(shape, lambda *_: (0,) * n)


def _params(sem):
    return pltpu.CompilerParams(dimension_semantics=sem, vmem_limit_bytes=VMEM_LIMIT)


def _head_blockdiag(n, blk):
    r = jnp.arange(n) // blk
    return (r[:, None] == r[None, :]).astype(BF16)


def _inproj_body(x_ref, g_ref, w_ref, wab_ref, bd_ref, gq_ref, gk_ref,
                 qkv_ref, z_ref, ab_ref, q_ref, k_ref, v_ref, k16_ref, v16_ref, *, seq_minor):
    x = x_ref[...]
    ms = jnp.mean(x * x, axis=-1, keepdims=True)
    u = (x * lax.rsqrt(ms + NORM_EPS) * g_ref[...]).astype(BF16)
    qkv_ref[...] = _dot(u, w_ref[:, 0:CONV_DIM])
    z_ref[...] = _dot(u, w_ref[:, CONV_DIM:CONV_DIM + WIDTH])
    ab_ref[...] = _dot(u, wab_ref[...])
    o = CONV_DIM + WIDTH
    bd = bd_ref[...]

    def headnorm(t, g):
        ss = _head_sumsq(t, bd)
        return t * lax.rsqrt(ss * (1.0 / HEAD_DIM) + NORM_EPS) * g

    q = headnorm(_dot(u, w_ref[:, o:o + WIDTH]), gq_ref[...])
    k = headnorm(_dot(u, w_ref[:, o + WIDTH:o + 2 * WIDTH]), gk_ref[...])
    v = _dot(u, w_ref[:, o + 2 * WIDTH:o + 3 * WIDTH])
    q_ref[...] = (q * (SB_SCALE * LOG2E)).astype(BF16)
    tm = k.shape[0]
    if seq_minor:
        k_ref[...] = k.T
        v_ref[...] = v.T
    else:
        for h in range(N_HEADS):
            k_ref[pl.ds(h, tm, stride=N_HEADS), :] = k[:, h * HEAD_DIM:(h + 1) * HEAD_DIM]
            v_ref[pl.ds(h, tm, stride=N_HEADS), :] = v[:, h * HEAD_DIM:(h + 1) * HEAD_DIM]
    k16_ref[...] = k.astype(BF16)
    v16_ref[...] = v.astype(BF16)


def _inproj(x2d, g_mix, w_main, w_ab, bd, gq, gk, tm, seq):
    n = x2d.shape[0]
    seq_minor = seq % tm == 0
    row = lambda w: pl.BlockSpec((tm, w), lambda i: (i, 0))
    if seq_minor:
        tiles = seq // tm
        kv_spec = pl.BlockSpec((None, WIDTH, tm), lambda i: (i // tiles, 0, i % tiles))
        kv_shape = jax.ShapeDtypeStruct((n // seq, WIDTH, seq), F32)
    else:
        kv_spec = pl.BlockSpec((tm * N_HEADS, HEAD_DIM), lambda i: (i, 0))
        kv_shape = jax.ShapeDtypeStruct((n * N_HEADS, HEAD_DIM), F32)
    outs = [(CONV_DIM, F32), (WIDTH, F32), (GATE_PAD, F32), (WIDTH, BF16), None, None, (WIDTH, BF16), (WIDTH, BF16)]
    return pl.pallas_call(
        functools.partial(_inproj_body, seq_minor=seq_minor),
        grid=(n // tm,),
        in_specs=[row(D_MODEL), _const_spec((1, D_MODEL)), _const_spec(w_main.shape), _const_spec(w_ab.shape),
                  _const_spec(bd.shape), _const_spec((1, WIDTH)), _const_spec((1, WIDTH))],
        out_specs=[kv_spec if o is None else row(o[0]) for o in outs],
        out_shape=[kv_shape if o is None else jax.ShapeDtypeStruct((n, o[0]), o[1]) for o in outs],
        compiler_params=_params(("parallel",)),
        name="inproj",
    )(x2d, g_mix, w_main, w_ab, bd, gq, gk)


DN_CHUNKS = 4
DN_STREAMS = 8
DN_PRECISE_FACTORS = 3


def _lane_head0(shape):
    return lax.broadcasted_iota(jnp.int32, shape, len(shape) - 1) % PAIR < HEAD_DIM


def _bdiag(x):
    m0 = _lane_head0(x.shape)
    zero = jnp.zeros((), x.dtype)
    return jnp.concatenate([jnp.where(m0, x, zero), jnp.where(m0, zero, x)], axis=0)


def _dn_prep_body(qkv_ref, halo_ref, ab_ref, cs_ref, cw_ref, alog_ref, dtb_ref, bdw_ref, expand_ref,
                  u_ref, w_ref, qe_ref, ks_ref, qk_ref, egl_ref,
                  xp_ref, y_ref, gb_ref, *, seg_rows, n_seg, blocks_per_stream):
    L = CHUNK
    pad = max(seg_rows, L)
    R = n_seg * pad
    nc = R // L

    for s in range(n_seg):
        if blocks_per_stream:
            xp_ref[s, 5:8, :] = halo_ref[5:8, :]

            @pl.when(pl.program_id(0) % blocks_per_stream == 0)
            def _():
                xp_ref[s, 5:8, :] = cs_ref[...]
        else:
            xp_ref[s, 5:8, :] = cs_ref[s]
        xp_ref[s, 8:8 + seg_rows, :] = qkv_ref[s * seg_rows:(s + 1) * seg_rows, :]
        y = xp_ref[s, 5:5 + seg_rows, :] * cw_ref[0:1, :]
        for i in range(1, CONV_K):
            y = y + xp_ref[s, 5 + i:5 + i + seg_rows, :] * cw_ref[i:i + 1, :]
        y_ref[s * pad:s * pad + seg_rows, :] = _silu(y)
        ab = ab_ref[s * seg_rows:(s + 1) * seg_rows, :]
        col = lax.broadcasted_iota(jnp.int32, ab.shape, 1)
        g = -jnp.exp(alog_ref[...]) * _softplus(ab + dtb_ref[...])
        gb_ref[s * pad:s * pad + seg_rows, :] = jnp.where(col < N_HEADS, g, _sigmoid(ab))
        if seg_rows < pad:
            y_ref[s * pad + seg_rows:(s + 1) * pad, :] = jnp.zeros((pad - seg_rows, CONV_DIM), F32)
            gb_ref[s * pad + seg_rows:(s + 1) * pad, :] = jnp.zeros((pad - seg_rows, GATE_PAD), F32)

    bdw = bdw_ref[...]
    q = y_ref[:, 0:WIDTH]
    k = y_ref[:, WIDTH:2 * WIDTH]
    q = q * lax.rsqrt(_head_sumsq(q, bdw) + NORM_EPS) * SB_SCALE
    k = k * lax.rsqrt(_head_sumsq(k, bdw) + NORM_EPS)

    rr = lax.broadcasted_iota(jnp.int32, (R, R), 0)
    cc = lax.broadcasted_iota(jnp.int32, (R, R), 1)
    same = (rr // L) == (cc // L)
    blk_lower = (same & (rr >= cc)).astype(BF16)
    blk_ones = same.astype(BF16)
    gb = gb_ref[...]
    colg = lax.broadcasted_iota(jnp.int32, (R, GATE_PAD), 1)
    gcum = jnp.where(colg < N_HEADS, _mm(_dot, blk_lower, gb, 1, 3), gb)
    ex = _mm(_dot, gcum, expand_ref[...], 3, 1)
    gc_all = ex[:, 0:WIDTH]
    beta_all = ex[:, WIDTH:2 * WIDTH]
    r512 = lax.broadcasted_iota(jnp.int32, (R, WIDTH), 0)
    l512 = lax.broadcasted_iota(jnp.int32, (R, WIDTH), 1)
    eye = (r512 % L) == (l512 % HEAD_DIM)
    gct_all = _mm(_dot, blk_ones, jnp.where(eye, gc_all, 0.0), 1, 2)

    row = lax.broadcasted_iota(jnp.int32, (L, PAIR), 0)
    jcol = lax.broadcasted_iota(jnp.int32, (L, PAIR), 1) % HEAD_DIM
    causal = row >= jcol
    strict = row > jcol

    chains = [(slice(c * L, (c + 1) * L), slice(p * PAIR, (p + 1) * PAIR)) for c in range(nc) for p in range(N_PAIRS)]
    for c in range(nc):
        egl_ref[c] = jnp.exp(gc_all[(c + 1) * L - 1:(c + 1) * L, :])
    a_list, x_list = [], []
    for rs, sl in chains:
        qc = q[rs, sl]
        kc = k[rs, sl]
        vc = y_ref[rs, 2 * WIDTH + sl.start:2 * WIDTH + sl.stop]
        gc = gc_all[rs, sl]
        beta = beta_all[rs, sl]
        eg = jnp.exp(gc)
        decay = jnp.where(causal, jnp.exp(jnp.where(causal, gc - gct_all[rs, sl], 0.0)), 0.0)
        kb = kc * beta
        kst = _bdiag(kc.astype(BF16))
        aq = _dot_nt(jnp.concatenate([kb, qc], axis=0).astype(BF16), kst)
        a_list.append(jnp.where(strict, aq[0:L] * decay, 0.0))
        x_list.append(jnp.concatenate([vc * beta, kb * eg], axis=1))
        qk_ref[rs, sl] = jnp.where(causal, aq[L:2 * L] * decay, 0.0).astype(BF16)
        qe_ref[rs, sl] = (qc * eg).astype(BF16)
        ks_ref[rs, sl] = (kc * jnp.exp(gc[L - 1:L, :] - gc)).astype(BF16)
    for t in range(6):
        np_ = 2 if t < DN_PRECISE_FACTORS else 1
        for n in range(len(chains)):
            ak, x = a_list[n], x_list[n]
            if np_ == 1:
                ak, x = ak.astype(BF16), x.astype(BF16)
            if t < 5:
                r = _mm(_dot, ak, jnp.concatenate([_bdiag(ak), _bdiag(x)], axis=1), np_, np_)
                a_list[n], ax = r[:, 0:PAIR], r[:, PAIR:]
            else:
                ax = _mm(_dot, ak, _bdiag(x), np_, np_)
            x_list[n] = x_list[n] - ax if t == 0 else x_list[n] + ax
    for (rs, sl), x in zip(chains, x_list):
        u_ref[rs, sl] = x[:, 0:PAIR]
        w_ref[rs, sl] = x[:, PAIR:].astype(BF16)


def _dn_prep(qkv, ab, conv_state, wts, n_streams, seq):
    L = CHUNK
    if seq >= L:
        seg_rows, n_seg = DN_CHUNKS * L, 1
        blocks_per_stream = seq // seg_rows
        cs_spec = pl.BlockSpec((None, CONV_K - 1, CONV_DIM), lambda i: (i // blocks_per_stream, 0, 0))
    else:
        seg_rows, n_seg, blocks_per_stream = seq, DN_CHUNKS, 0
        cs_spec = pl.BlockSpec((n_seg, CONV_K - 1, CONV_DIM), lambda i: (i, 0, 0))
    rows_in = n_seg * seg_rows
    R = n_seg * max(seg_rows, L)
    n_blocks = n_streams * seq // rows_in
    tok = lambda w: pl.BlockSpec((rows_in, w), lambda i: (i, 0))
    halo = pl.BlockSpec((8, CONV_DIM), lambda i: (jnp.maximum(i * (rows_in // 8) - 1, 0), 0))
    out = lambda: pl.BlockSpec((R, WIDTH), lambda i: (i, 0))
    n_rows = n_blocks * R
    body = functools.partial(_dn_prep_body, seg_rows=seg_rows, n_seg=n_seg, blocks_per_stream=blocks_per_stream)
    return pl.pallas_call(
        body,
        grid=(n_blocks,),
        in_specs=[tok(CONV_DIM), halo, tok(GATE_PAD), cs_spec,
                  _const_spec((CONV_K, CONV_DIM)), _const_spec((1, GATE_PAD)), _const_spec((1, GATE_PAD)),
                  _const_spec((WIDTH, WIDTH)), _const_spec((GATE_PAD, 2 * WIDTH))],
        out_specs=[out(), out(), out(), out(), out(), pl.BlockSpec((R // L, 1, WIDTH), lambda i: (i, 0, 0))],
        out_shape=[jax.ShapeDtypeStruct((n_rows, WIDTH), F32)] +
                  [jax.ShapeDtypeStruct((n_rows, WIDTH), BF16)] * 4 +
                  [jax.ShapeDtypeStruct((n_rows // L, 1, WIDTH), F32)],
        scratch_shapes=[pltpu.VMEM((n_seg, seg_rows + 8, CONV_DIM), F32), pltpu.VMEM((R, CONV_DIM), F32),
                        pltpu.VMEM((R, GATE_PAD), F32)],
        compiler_params=_params(("parallel",)),
        name="dn_prep",
    )(qkv, qkv, ab, conv_state, wts["conv_w"], wts["alog"], wts["dtb"], wts["bd_w"], wts["expand"])


def _dn_scan_body(u_ref, w_ref, qe_ref, ks_ref, qk_ref, egl_ref, z_ref, s0_ref, gout_ref, bdw_ref,
                  o_ref, s_out_ref, s_ref, oraw_ref, *, rows, nb, n_steps):
    c = pl.program_id(1)
    L = CHUNK

    @pl.when(c == 0)
    def _():
        s_ref[...] = s0_ref[...]

    r = lax.broadcasted_iota(jnp.int32, (PAIR, PAIR), 0)
    cl = lax.broadcasted_iota(jnp.int32, (PAIR, PAIR), 1)
    same_head = (r // HEAD_DIM) == (cl // HEAD_DIM)
    chains = [(b, p, slice(p * PAIR, (p + 1) * PAIR)) for b in range(nb) for p in range(N_PAIRS)]
    group = 4 * N_PAIRS
    for g0 in range(0, len(chains), group):
        grp = chains[g0:g0 + group]
        ws = [_dot(jnp.concatenate([w_ref[b, :, sl], qe_ref[b, :, sl]], axis=0), s_ref[b, p].astype(BF16))
              for b, p, sl in grp]
        v_new = [(u_ref[b, :, sl] - r[0:L]).astype(BF16) for (b, p, sl), r in zip(grp, ws)]
        for (b, p, sl), r, v in zip(grp, ws, v_new):
            oraw_ref[b, :, sl] = r[L:2 * L] + _dot(qk_ref[b, :, sl], _bdiag(v))
        for (b, p, sl), v in zip(grp, v_new):
            s_ref[b, p] = (s_ref[b, p] * egl_ref[b, 0, :, sl]
                           + jnp.where(same_head, _dot_tn(ks_ref[b, :, sl], v), 0.0))

    o = oraw_ref[...].reshape(nb * L, WIDTH)
    o = o * lax.rsqrt(_head_sumsq(o, bdw_ref[...]) * (1.0 / HEAD_DIM) + NORM_EPS) * gout_ref[...]
    for b in range(nb):
        o_ref[b] = o[b * L:b * L + rows] * _silu(z_ref[b])

    @pl.when(c == n_steps - 1)
    def _():
        s_out_ref[...] = s_ref[...]


def _dn_scan(prep, z, s0_bd, wts, n_streams, seq):
    L = CHUNK
    rows = min(L, seq)
    n_steps = seq // rows
    nb = DN_STREAMS
    u, w, qe, ks, qk, egl = prep
    as3 = lambda a: a.reshape(n_streams, n_steps * L, WIDTH)
    blk = lambda: pl.BlockSpec((nb, L, WIDTH), lambda g, c: (g, c, 0))
    tok = lambda: pl.BlockSpec((nb, rows, WIDTH), lambda g, c: (g, c, 0))
    st = lambda: pl.BlockSpec((nb, N_PAIRS, PAIR, PAIR), lambda g, c: (g, 0, 0, 0))
    body = functools.partial(_dn_scan_body, rows=rows, nb=nb, n_steps=n_steps)
    o, s_new = pl.pallas_call(
        body,
        grid=(n_streams // nb, n_steps),
        in_specs=[blk(), blk(), blk(), blk(), blk(),
                  pl.BlockSpec((nb, 1, 1, WIDTH), lambda g, c: (g, c, 0, 0)), tok(), st(),
                  _const_spec((1, WIDTH)), _const_spec((WIDTH, WIDTH))],
        out_specs=[tok(), st()],
        out_shape=[jax.ShapeDtypeStruct((n_streams, seq, WIDTH), F32),
                   jax.ShapeDtypeStruct((n_streams, N_PAIRS, PAIR, PAIR), F32)],
        scratch_shapes=[pltpu.VMEM((nb, N_PAIRS, PAIR, PAIR), F32), pltpu.VMEM((nb, L, WIDTH), F32)],
        compiler_params=_params(("parallel", "arbitrary")),
        name="dn_scan",
    )(as3(u), as3(w), as3(qe), as3(ks), as3(qk), egl.reshape(n_streams, n_steps, 1, WIDTH),
      z.reshape(n_streams, seq, WIDTH), s0_bd, wts["gout"], wts["bd_w"])
    return o.reshape(n_streams * seq, WIDTH), s_new


def _sb_blocks(probs, tri, valid, keys_minor=False):
    zs = [(_dot if keys_minor else _dot_nt)(q, k) for q, k, _, _ in probs]
    keeps = []
    for z in zs:
        nz = -z
        log_keep = jnp.minimum(nz, 0.0) - jnp.log2(1.0 + jnp.exp2(jnp.minimum(z, nz)))
        if valid is not None:
            log_keep = jnp.where(valid, log_keep, 0.0)
        keeps.append(log_keep.astype(BF16))
    sums = [_dot(lk, tri) for lk in keeps]
    out = []
    for (_, _, v, carry), z, sm in zip(probs, zs, sums):
        if isinstance(carry, int):
            carry = out[carry][1]
        a = jnp.exp2(z + sm)
        if valid is not None:
            a = jnp.where(valid, a, 0.0)
        pv = (_dot_nt if keys_minor else _dot)(a.astype(BF16), v) * jnp.exp2(carry)
        out.append((pv, carry + sm[:, 0:1]))
    return out


def _sb_tri(n):
    r = lax.broadcasted_iota(jnp.int32, (n, n), 0)
    c = lax.broadcasted_iota(jnp.int32, (n, n), 1)
    return (r >= c).astype(BF16)


def _sb_prompt_body(q_ref, k_ref, v_ref, o_ref):
    i = pl.program_id(1)
    t = SB_BLOCK
    tri = _sb_tri(t)
    r = lax.broadcasted_iota(jnp.int32, (2 * t, t), 0)
    c = lax.broadcasted_iota(jnp.int32, (2 * t, t), 1)
    strict = c < (r & (t - 1))
    head0 = _lane_head0((t, PAIR))
    zero = jnp.zeros((2 * t, 1), F32)
    for p0 in range(0, N_PAIRS, SB_PAIRS):
        sls = [slice(p * PAIR, (p + 1) * PAIR) for p in range(p0, p0 + SB_PAIRS)]
        qs = [_bdiag(q_ref[:, sl]) for sl in sls]

        def probs(j, carries):
            off = pl.multiple_of(j * t, t)
            return [(q, k_ref[pl.ds(off, t), sl], v_ref[pl.ds(off, t), sl], c) for q, sl, c in zip(qs, sls, carries)]

        first = _sb_blocks(probs(i, [zero] * SB_PAIRS), tri, strict)

        def step2(n, st):
            j = i - 1 - 2 * n
            res = _sb_blocks(probs(j, [c for _, c in st]) + probs(j - 1, list(range(SB_PAIRS))), tri, None)
            return tuple((acc + pv0 + pv1, c1)
                         for (acc, _), (pv0, _), (pv1, c1) in zip(st, res[:SB_PAIRS], res[SB_PAIRS:]))

        def step1(n, st):
            res = _sb_blocks(probs(0, [c for _, c in st]), tri, None)
            return tuple((acc + pv, c) for (acc, _), (pv, c) in zip(st, res))

        last = lax.fori_loop(0, i // 2, step2, tuple(first))
        last = lax.fori_loop(0, i % 2, step1, last)
        for sl, (acc, _) in zip(sls, last):
            o_ref[:, sl] = jnp.where(head0, acc[0:t], acc[t:2 * t])


def _sb_prompt(q16, k16, v16, batch, seq):
    t = SB_BLOCK
    nq = seq // t
    q3 = q16.reshape(batch, seq, WIDTH)
    k3 = k16.reshape(batch, seq, WIDTH)
    v3 = v16.reshape(batch, seq, WIDTH)
    full = pl.BlockSpec((None, seq, WIDTH), lambda b, i: (b, 0, 0))
    blk = pl.BlockSpec((None, t, WIDTH), lambda b, i: (b, i, 0))
    out = pl.pallas_call(
        _sb_prompt_body,
        grid=(batch, nq),
        in_specs=[blk, full, full],
        out_specs=blk,
        out_shape=jax.ShapeDtypeStruct((batch, seq, WIDTH), F32),
        compiler_params=_params(("parallel", "arbitrary")),
        name="sb_prompt",
    )(q3, k3, v3)
    return out.reshape(batch * seq, WIDTH)


def _sb_sample_body(q_ref, kn_ref, vn_ref, kp_ref, vp_ref, o_ref, qs_ref, acc_ref, carry_ref, *, tq, tk, n_steps):
    j = pl.program_id(1)
    t = SB_BLOCK
    m = N_HEADS * tq
    tri = _sb_tri(t)

    @pl.when(j == 0)
    def _():
        q = q_ref[...]
        lane = lax.broadcasted_iota(jnp.int32, (tq, WIDTH), 1)
        for h in range(N_HEADS):
            hm = (lane >= h * HEAD_DIM) & (lane < (h + 1) * HEAD_DIM)
            qs_ref[h * tq:(h + 1) * tq, :] = jnp.where(hm, q, jnp.zeros((), BF16))
        kn = jnp.concatenate([kn_ref[...], jnp.zeros((t - tq, WIDTH), BF16)], axis=0)
        vn = jnp.concatenate([vn_ref[...], jnp.zeros((t - tq, WIDTH), BF16)], axis=0)
        r = lax.broadcasted_iota(jnp.int32, (m, t), 0)
        c = lax.broadcasted_iota(jnp.int32, (m, t), 1)
        valid = c < (r & (tq - 1))
        (pv, carry), = _sb_blocks([(qs_ref[...], kn, vn, jnp.zeros((m, 1), F32))], tri, valid)
        acc_ref[...] = pv
        carry_ref[...] = carry

    qs = qs_ref[...]
    subs = list(reversed(range(tk // t)))
    res = _sb_blocks([(qs, kp_ref[:, s * t:(s + 1) * t].astype(BF16), vp_ref[:, s * t:(s + 1) * t].astype(BF16),
                       carry_ref[...] if n == 0 else n - 1) for n, s in enumerate(subs)],
                     tri, None, keys_minor=True)
    acc = acc_ref[...]
    for pv, _ in res:
        acc = acc + pv
    acc_ref[...] = acc
    carry_ref[...] = res[-1][1]

    @pl.when(j == n_steps - 1)
    def _():
        lane = lax.broadcasted_iota(jnp.int32, (tq, WIDTH), 1)
        out = jnp.zeros((tq, WIDTH), F32)
        for h in range(N_HEADS):
            hm = (lane >= h * HEAD_DIM) & (lane < (h + 1) * HEAD_DIM)
            out = out + jnp.where(hm, acc[h * tq:(h + 1) * tq, :], 0.0)
        o_ref[...] = out


def _sb_sample(q16, k16, v16, past_k, past_v, batch, tq, tk):
    past = past_k.shape[2]
    n_steps = past // tk
    assert tq & (tq - 1) == 0 and tq <= SB_BLOCK
    new = lambda: pl.BlockSpec((None, tq, WIDTH), lambda b, j: (b, 0, 0))
    old = lambda: pl.BlockSpec((None, WIDTH, tk), lambda b, j: (b, 0, n_steps - 1 - j))
    m = N_HEADS * tq
    body = functools.partial(_sb_sample_body, tq=tq, tk=tk, n_steps=n_steps)
    out = pl.pallas_call(
        body,
        grid=(batch, n_steps),
        in_specs=[new(), new(), new(), old(), old()],
        out_specs=new(),
        out_shape=jax.ShapeDtypeStruct((batch, tq, WIDTH), F32),
        scratch_shapes=[pltpu.VMEM((m, WIDTH), BF16), pltpu.VMEM((m, WIDTH), F32), pltpu.VMEM((m, 1), F32)],
        compiler_params=_params(("parallel", "arbitrary")),
        name="sb_sample",
    )(q16.reshape(batch, tq, WIDTH), k16.reshape(batch, tq, WIDTH), v16.reshape(batch, tq, WIDTH),
      past_k, past_v)
    return out.reshape(batch * tq, WIDTH)


def _post_body(x_ref, oa_ref, ob_ref, p_ref, wo_ref, gm_ref, wu_ref, wd_ref, gp_ref, wg_ref, wp_ref, y_ref):
    def rms(h, g):
        ms = jnp.mean(h * h, axis=-1, keepdims=True)
        return (h * lax.rsqrt(ms + NORM_EPS) * g).astype(BF16)

    y_ref[...] = x_ref[...] + _dot(oa_ref[...].astype(BF16), wo_ref[0:WIDTH, :]) \
        + _dot(ob_ref[...].astype(BF16), wo_ref[WIDTH:2 * WIDTH, :])
    h = y_ref[...]
    u = rms(h, gm_ref[...])
    ff = D_MODEL
    for j in range(D_FF // ff):
        up = jnp.maximum(_dot(u, wu_ref[:, j * ff:(j + 1) * ff]), 0.0)
        h = h + _dot((up * up).astype(BF16), wd_ref[j * ff:(j + 1) * ff, :])
    y_ref[...] = h
    h = y_ref[...]
    gate = _sigmoid(_dot(rms(h, gp_ref[...]), wg_ref[...]))
    y_ref[...] = h + gate * _dot(p_ref[...].astype(BF16), wp_ref[...])


def _post(x2d, oa, ob, p2d, w_out, g_mlp, w_up, w_down, g_ple, w_gate, w_proj, tm):
    n = x2d.shape[0]
    row = lambda w: pl.BlockSpec((tm, w), lambda i: (i, 0))
    cs = lambda a: pl.BlockSpec(a.shape, lambda i: (0, 0), pipeline_mode=pl.Buffered(1))
    return pl.pallas_call(
        _post_body,
        grid=(n // tm,),
        in_specs=[row(D_MODEL), row(WIDTH), row(WIDTH), row(PLE_DIM), cs(w_out), cs(g_mlp), cs(w_up), cs(w_down),
                  cs(g_ple), cs(w_gate), cs(w_proj)],
        out_specs=row(D_MODEL),
        out_shape=jax.ShapeDtypeStruct((n, D_MODEL), F32),
        compiler_params=_params(("parallel",)),
        name="post",
    )(x2d, oa, ob, p2d, w_out, g_mlp, w_up, w_down, g_ple, w_gate, w_proj)


def _pair_blockdiag(s):
    b = s.shape[0]
    s = s.reshape(b, N_PAIRS, 2, HEAD_DIM, HEAD_DIM)
    zero = jnp.zeros_like(s[:, :, 0])
    top = jnp.concatenate([s[:, :, 0], zero], axis=-1)
    bot = jnp.concatenate([zero, s[:, :, 1]], axis=-1)
    return jnp.concatenate([top, bot], axis=-2)


def _pair_unblock(s):
    b = s.shape[0]
    h0 = s[:, :, :HEAD_DIM, :HEAD_DIM]
    h1 = s[:, :, HEAD_DIM:, HEAD_DIM:]
    return jnp.stack([h0, h1], axis=2).reshape(b, N_HEADS, HEAD_DIM, HEAD_DIM)


def _expand_matrix():
    lane = jnp.arange(2 * WIDTH)
    src = (lane // WIDTH) * N_HEADS + (lane % WIDTH) // HEAD_DIM
    return (jnp.arange(GATE_PAD)[:, None] == src[None, :]).astype(BF16)


def _seq_minor(a):
    b, sq, h, d = a.shape
    return a.transpose(0, 2, 3, 1).reshape(b, h * d, sq)


def _group(x, p, conv_state, s0, past_k, past_v, wts, tm, sb_tk):
    batch, seq, _ = x.shape
    n = batch * seq
    x2d = x.reshape(n, D_MODEL)
    qkv, z, ab, q16, k, v, k16, v16 = _inproj(x2d, wts["g_mix"], wts["w_main"], wts["w_ab"], wts["bd_w"],
                                              wts["gq"], wts["gk"], tm, seq)
    prep = _dn_prep(qkv, ab, conv_state, wts, batch, seq)
    o_a, s_new = _dn_scan(prep, z, _pair_blockdiag(s0), wts, batch, seq)
    new_conv = qkv.reshape(batch, seq, CONV_DIM)[:, seq - (CONV_K - 1):, :]
    if past_k is None:
        o_b = _sb_prompt(q16, k16, v16, batch, seq)
    else:
        o_b = _sb_sample(q16, k16, v16, past_k, past_v, batch, seq, sb_tk)
    y = _post(x2d, o_a, o_b, p.reshape(n, PLE_DIM), wts["w_out"], wts["g_mlp"], wts["w_up"], wts["w_down"],
              wts["g_ple"], wts["w_gate"], wts["w_proj"], tm)
    if k.ndim == 3:
        heads = lambda a: a.reshape(batch, N_HEADS, HEAD_DIM, seq).transpose(0, 3, 1, 2)
    else:
        heads = lambda a: a.reshape(batch, seq, N_HEADS, HEAD_DIM)
    return y.reshape(batch, seq, D_MODEL), new_conv, _pair_unblock(s_new), heads(k), heads(v)


def _layer_weights(i, g_mix, w_in, conv_w, a_log, dt_bias, g_out_dn, g_q_sb, g_k_sb, w_out, g_mlp, w_up, w_down,
                   g_ple, w_ple_gate, w_ple_proj):
    w = w_in[i]
    gate0 = CONV_DIM + WIDTH
    sb0 = gate0 + 2 * N_HEADS
    pad_row = lambda a: jnp.pad(a.astype(F32), (0, GATE_PAD - a.shape[0])).reshape(1, GATE_PAD)
    return {
        "g_mix": g_mix[i].reshape(1, D_MODEL),
        "w_main": jnp.concatenate([w[:, :gate0], w[:, sb0:]], axis=1).astype(BF16),
        "w_ab": jnp.pad(w[:, gate0:sb0], ((0, 0), (0, GATE_PAD - 2 * N_HEADS))).astype(BF16),
        "bd_w": _head_blockdiag(WIDTH, HEAD_DIM),
        "expand": _expand_matrix(),
        "gq": jnp.tile(g_q_sb[i], N_HEADS).reshape(1, WIDTH),
        "gk": jnp.tile(g_k_sb[i], N_HEADS).reshape(1, WIDTH),
        "conv_w": conv_w[i],
        "alog": pad_row(a_log[i]),
        "dtb": pad_row(dt_bias[i]),
        "gout": jnp.tile(g_out_dn[i], N_HEADS).reshape(1, WIDTH),
        "w_out": w_out[i].astype(BF16),
        "g_mlp": g_mlp[i].reshape(1, D_MODEL),
        "w_up": w_up[i].astype(BF16),
        "w_down": w_down[i].astype(BF16),
        "g_ple": g_ple[i].reshape(1, D_MODEL),
        "w_gate": w_ple_gate[i].astype(BF16),
        "w_proj": w_ple_proj[i].astype(BF16),
    }


def kernel(x_prompt, x_sample, cache_conv, state_delta, cache_k, cache_v, p_prompt, p_sample, g_mix, w_in, conv_w, a_log, dt_bias, g_out_dn, g_q_sb, g_k_sb, w_out, g_mlp, w_up, w_down, g_ple, w_ple_gate, w_ple_proj):
    depth = w_in.shape[0]
    bp = x_prompt.shape[0]
    y_p, y_s = x_prompt, x_sample
    outs_p, outs_s = [], []
    for i in range(depth):
        wts = _layer_weights(i, g_mix, w_in, conv_w, a_log, dt_bias, g_out_dn, g_q_sb, g_k_sb, w_out, g_mlp,
                             w_up, w_down, g_ple, w_ple_gate, w_ple_proj)
        y_p, *rest = _group(y_p, p_prompt[i], jnp.zeros((bp, CONV_K - 1, CONV_DIM), F32),
                            jnp.zeros((bp, N_HEADS, HEAD_DIM, HEAD_DIM), F32), None, None, wts,
                            tm=512, sb_tk=None)
        outs_p.append(rest)
        dec_b, past_len = cache_k.shape[1], cache_k.shape[2]
        y_s, *rest = _group(y_s, p_sample[i], cache_conv[i], state_delta[i],
                            _seq_minor(cache_k[i]), _seq_minor(cache_v[i]),
                            wts, tm=512, sb_tk=min(1024, past_len))
        outs_s.append(rest)
    stack = lambda outs, j: jnp.stack([o[j] for o in outs])
    return (y_p, y_s,
            stack(outs_p, 0), stack(outs_p, 1), stack(outs_p, 2), stack(outs_p, 3),
            stack(outs_s, 0), stack(outs_s, 1), stack(outs_s, 2), stack(outs_s, 3))
```

```python
import functools
import math

import jax
import jax.numpy as jnp
from jax import lax
from jax.experimental import pallas as pl
from jax.experimental.pallas import tpu as pltpu

F32 = jnp.float32
BF16 = jnp.bfloat16

D_MODEL = 1024
HEAD_DIM = 64
N_HEADS = 8
WIDTH = N_HEADS * HEAD_DIM
N_PAIRS = N_HEADS // 2
PAIR = 2 * HEAD_DIM
CONV_K = 4
CONV_DIM = 3 * WIDTH
CHUNK = 64
D_FF = 4 * D_MODEL
PLE_DIM = 256
SB_SCALE = HEAD_DIM ** -0.5
LOG2E = math.log2(math.e)
NORM_EPS = 1e-6
GATE_PAD = 128
SB_BLOCK = 256
SB_PAIRS = 4
SB_DEAD = -160.0
VMEM_LIMIT = 56 * 1024 * 1024


def _dot(a, b):
    return lax.dot_general(a, b, (((1,), (0,)), ((), ())), preferred_element_type=F32)


def _dot_nt(a, b):
    return lax.dot_general(a, b, (((1,), (1,)), ((), ())), preferred_element_type=F32)


def _dot_tn(a, b):
    return lax.dot_general(a, b, (((0,), (0,)), ((), ())), preferred_element_type=F32)


def _split(x, n):
    parts = []
    r = x
    for i in range(n):
        p = r.astype(BF16)
        parts.append(p)
        if i + 1 < n:
            r = r - p.astype(F32)
    return parts


def _mm(dot, a, b, na, nb):
    ap = [a] if a.dtype == BF16 else _split(a, na)
    bp = [b] if b.dtype == BF16 else _split(b, nb)
    out = None
    for i, x in enumerate(ap):
        for j, y in enumerate(bp):
            if i + j >= max(len(ap), len(bp)):
                continue
            t = dot(x, y)
            out = t if out is None else out + t
    return out


def _head_sumsq(t, bd):
    return _dot((t * t).astype(BF16), bd)


def _softplus(x):
    return jnp.maximum(x, 0.0) + jnp.log1p(jnp.exp(-jnp.abs(x)))


def _sigmoid(x):
    return 1.0 / (1.0 + jnp.exp(-x))


def _silu(x):
    return x * _sigmoid(x)


def _const_spec(shape):
    n = len(shape)
    return pl.BlockSpec(shape, lambda *_: (0,) * n)


def _params(sem):
    return pltpu.CompilerParams(dimension_semantics=sem, vmem_limit_bytes=VMEM_LIMIT)


def _head_blockdiag(n, blk):
    r = jnp.arange(n) // blk
    return (r[:, None] == r[None, :]).astype(BF16)


def _inproj_body(x_ref, g_ref, w_ref, wab_ref, bd_ref, gq_ref, gk_ref,
                 qkv_ref, z_ref, ab_ref, q_ref, k_ref, v_ref, k16_ref, v16_ref, *, seq_minor):
    x = x_ref[...]
    ms = jnp.mean(x * x, axis=-1, keepdims=True)
    u = (x * lax.rsqrt(ms + NORM_EPS) * g_ref[...]).astype(BF16)
    qkv_ref[...] = _dot(u, w_ref[:, 0:CONV_DIM])
    z_ref[...] = _dot(u, w_ref[:, CONV_DIM:CONV_DIM + WIDTH])
    ab_ref[...] = _dot(u, wab_ref[...])
    o = CONV_DIM + WIDTH
    bd = bd_ref[...]

    def headnorm(t, g):
        ss = _head_sumsq(t, bd)
        return t * lax.rsqrt(ss * (1.0 / HEAD_DIM) + NORM_EPS) * g

    q = headnorm(_dot(u, w_ref[:, o:o + WIDTH]), gq_ref[...])
    k = headnorm(_dot(u, w_ref[:, o + WIDTH:o + 2 * WIDTH]), gk_ref[...])
    v = _dot(u, w_ref[:, o + 2 * WIDTH:o + 3 * WIDTH])
    q_ref[...] = (q * (SB_SCALE * LOG2E)).astype(BF16)
    tm = k.shape[0]
    if seq_minor:
        k_ref[...] = k.T
        v_ref[...] = v.T
    else:
        for h in range(N_HEADS):
            k_ref[pl.ds(h, tm, stride=N_HEADS), :] = k[:, h * HEAD_DIM:(h + 1) * HEAD_DIM]
            v_ref[pl.ds(h, tm, stride=N_HEADS), :] = v[:, h * HEAD_DIM:(h + 1) * HEAD_DIM]
    k16_ref[...] = k.astype(BF16)
    v16_ref[...] = v.astype(BF16)


def _inproj(x2d, g_mix, w_main, w_ab, bd, gq, gk, tm, seq):
    n = x2d.shape[0]
    seq_minor = seq % tm == 0
    row = lambda w: pl.BlockSpec((tm, w), lambda i: (i, 0))
    if seq_minor:
        tiles = seq // tm
        kv_spec = pl.BlockSpec((None, WIDTH, tm), lambda i: (i // tiles, 0, i % tiles))
        kv_shape = jax.ShapeDtypeStruct((n // seq, WIDTH, seq), F32)
    else:
        kv_spec = pl.BlockSpec((tm * N_HEADS, HEAD_DIM), lambda i: (i, 0))
        kv_shape = jax.ShapeDtypeStruct((n * N_HEADS, HEAD_DIM), F32)
    outs = [(CONV_DIM, F32), (WIDTH, F32), (GATE_PAD, F32), (WIDTH, BF16), None, None, (WIDTH, BF16), (WIDTH, BF16)]
    return pl.pallas_call(
        functools.partial(_inproj_body, seq_minor=seq_minor),
        grid=(n // tm,),
        in_specs=[row(D_MODEL), _const_spec((1, D_MODEL)), _const_spec(w_main.shape), _const_spec(w_ab.shape),
                  _const_spec(bd.shape), _const_spec((1, WIDTH)), _const_spec((1, WIDTH))],
        out_specs=[kv_spec if o is None else row(o[0]) for o in outs],
        out_shape=[kv_shape if o is None else jax.ShapeDtypeStruct((n, o[0]), o[1]) for o in outs],
        compiler_params=_params(("parallel",)),
        name="inproj",
    )(x2d, g_mix, w_main, w_ab, bd, gq, gk)


DN_CHUNKS = 4
DN_STREAMS = 8
DN_PRECISE_FACTORS = 3


def _lane_head0(shape):
    return lax.broadcasted_iota(jnp.int32, shape, len(shape) - 1) % PAIR < HEAD_DIM


def _bdiag(x):
    m0 = _lane_head0(x.shape)
    zero = jnp.zeros((), x.dtype)
    return jnp.concatenate([jnp.where(m0, x, zero), jnp.where(m0, zero, x)], axis=0)


def _dn_prep_body(qkv_ref, halo_ref, ab_ref, cs_ref, cw_ref, alog_ref, dtb_ref, bdw_ref, expand_ref,
                  u_ref, w_ref, qe_ref, ks_ref, qk_ref, egl_ref,
                  xp_ref, y_ref, gb_ref, *, seg_rows, n_seg, blocks_per_stream):
    L = CHUNK
    pad = max(seg_rows, L)
    R = n_seg * pad
    nc = R // L

    for s in range(n_seg):
        if blocks_per_stream:
            xp_ref[s, 5:8, :] = halo_ref[5:8, :]

            @pl.when(pl.program_id(0) % blocks_per_stream == 0)
            def _():
                xp_ref[s, 5:8, :] = cs_ref[...]
        else:
            xp_ref[s, 5:8, :] = cs_ref[s]
        xp_ref[s, 8:8 + seg_rows, :] = qkv_ref[s * seg_rows:(s + 1) * seg_rows, :]
        y = xp_ref[s, 5:5 + seg_rows, :] * cw_ref[0:1, :]
        for i in range(1, CONV_K):
            y = y + xp_ref[s, 5 + i:5 + i + seg_rows, :] * cw_ref[i:i + 1, :]
        y_ref[s * pad:s * pad + seg_rows, :] = _silu(y)
        ab = ab_ref[s * seg_rows:(s + 1) * seg_rows, :]
        col = lax.broadcasted_iota(jnp.int32, ab.shape, 1)
        g = -jnp.exp(alog_ref[...]) * _softplus(ab + dtb_ref[...])
        gb_ref[s * pad:s * pad + seg_rows, :] = jnp.where(col < N_HEADS, g, _sigmoid(ab))
        if seg_rows < pad:
            y_ref[s * pad + seg_rows:(s + 1) * pad, :] = jnp.zeros((pad - seg_rows, CONV_DIM), F32)
            gb_ref[s * pad + seg_rows:(s + 1) * pad, :] = jnp.zeros((pad - seg_rows, GATE_PAD), F32)

    bdw = bdw_ref[...]
    q = y_ref[:, 0:WIDTH]
    k = y_ref[:, WIDTH:2 * WIDTH]
    q = q * lax.rsqrt(_head_sumsq(q, bdw) + NORM_EPS) * SB_SCALE
    k = k * lax.rsqrt(_head_sumsq(k, bdw) + NORM_EPS)

    rr = lax.broadcasted_iota(jnp.int32, (R, R), 0)
    cc = lax.broadcasted_iota(jnp.int32, (R, R), 1)
    same = (rr // L) == (cc // L)
    blk_lower = (same & (rr >= cc)).astype(BF16)
    blk_ones = same.astype(BF16)
    gb = gb_ref[...]
    colg = lax.broadcasted_iota(jnp.int32, (R, GATE_PAD), 1)
    gcum = jnp.where(colg < N_HEADS, _mm(_dot, blk_lower, gb, 1, 3), gb)
    ex = _mm(_dot, gcum, expand_ref[...], 3, 1)
    gc_all = ex[:, 0:WIDTH]
    beta_all = ex[:, WIDTH:2 * WIDTH]
    r512 = lax.broadcasted_iota(jnp.int32, (R, WIDTH), 0)
    l512 = lax.broadcasted_iota(jnp.int32, (R, WIDTH), 1)
    eye = (r512 % L) == (l512 % HEAD_DIM)
    gct_all = _mm(_dot, blk_ones, jnp.where(eye, gc_all, 0.0), 1, 2)

    row = lax.broadcasted_iota(jnp.int32, (L, PAIR), 0)
    jcol = lax.broadcasted_iota(jnp.int32, (L, PAIR), 1) % HEAD_DIM
    causal = row >= jcol
    strict = row > jcol

    chains = [(slice(c * L, (c + 1) * L), slice(p * PAIR, (p + 1) * PAIR)) for c in range(nc) for p in range(N_PAIRS)]
    for c in range(nc):
        egl_ref[c] = jnp.exp(gc_all[(c + 1) * L - 1:(c + 1) * L, :])
    a_list, x_list = [], []
    for rs, sl in chains:
        qc = q[rs, sl]
        kc = k[rs, sl]
        vc = y_ref[rs, 2 * WIDTH + sl.start:2 * WIDTH + sl.stop]
        gc = gc_all[rs, sl]
        beta = beta_all[rs, sl]
        eg = jnp.exp(gc)
        decay = jnp.where(causal, jnp.exp(jnp.where(causal, gc - gct_all[rs, sl], 0.0)), 0.0)
        kb = kc * beta
        kst = _bdiag(kc.astype(BF16))
        aq = _dot_nt(jnp.concatenate([kb, qc], axis=0).astype(BF16), kst)
        a_list.append(jnp.where(strict, aq[0:L] * decay, 0.0))
        x_list.append(jnp.concatenate([vc * beta, kb * eg], axis=1))
        qk_ref[rs, sl] = jnp.where(causal, aq[L:2 * L] * decay, 0.0).astype(BF16)
        qe_ref[rs, sl] = (qc * eg).astype(BF16)
        ks_ref[rs, sl] = (kc * jnp.exp(gc[L - 1:L, :] - gc)).astype(BF16)
    for t in range(6):
        np_ = 2 if t < DN_PRECISE_FACTORS else 1
        for n in range(len(chains)):
            ak, x = a_list[n], x_list[n]
            if np_ == 1:
                ak, x = ak.astype(BF16), x.astype(BF16)
            if t < 5:
                r = _mm(_dot, ak, jnp.concatenate([_bdiag(ak), _bdiag(x)], axis=1), np_, np_)
                a_list[n], ax = r[:, 0:PAIR], r[:, PAIR:]
            else:
                ax = _mm(_dot, ak, _bdiag(x), np_, np_)
            x_list[n] = x_list[n] - ax if t == 0 else x_list[n] + ax
    for (rs, sl), x in zip(chains, x_list):
        u_ref[rs, sl] = x[:, 0:PAIR]
        w_ref[rs, sl] = x[:, PAIR:].astype(BF16)


def _dn_prep(qkv, ab, conv_state, wts, n_streams, seq):
    L = CHUNK
    if seq >= L:
        seg_rows, n_seg = DN_CHUNKS * L, 1
        blocks_per_stream = seq // seg_rows
        cs_spec = pl.BlockSpec((None, CONV_K - 1, CONV_DIM), lambda i: (i // blocks_per_stream, 0, 0))
    else:
        seg_rows, n_seg, blocks_per_stream = seq, DN_CHUNKS, 0
        cs_spec = pl.BlockSpec((n_seg, CONV_K - 1, CONV_DIM), lambda i: (i, 0, 0))
    rows_in = n_seg * seg_rows
    R = n_seg * max(seg_rows, L)
    n_blocks = n_streams * seq // rows_in
    tok = lambda w: pl.BlockSpec((rows_in, w), lambda i: (i, 0))
    halo = pl.BlockSpec((8, CONV_DIM), lambda i: (jnp.maximum(i * (rows_in // 8) - 1, 0), 0))
    out = lambda: pl.BlockSpec((R, WIDTH), lambda i: (i, 0))
    n_rows = n_blocks * R
    body = functools.partial(_dn_prep_body, seg_rows=seg_rows, n_seg=n_seg, blocks_per_stream=blocks_per_stream)
    return pl.pallas_call(
        body,
        grid=(n_blocks,),
        in_specs=[tok(CONV_DIM), halo, tok(GATE_PAD), cs_spec,
                  _const_spec((CONV_K, CONV_DIM)), _const_spec((1, GATE_PAD)), _const_spec((1, GATE_PAD)),
                  _const_spec((WIDTH, WIDTH)), _const_spec((GATE_PAD, 2 * WIDTH))],
        out_specs=[out(), out(), out(), out(), out(), pl.BlockSpec((R // L, 1, WIDTH), lambda i: (i, 0, 0))],
        out_shape=[jax.ShapeDtypeStruct((n_rows, WIDTH), F32)] +
                  [jax.ShapeDtypeStruct((n_rows, WIDTH), BF16)] * 4 +
                  [jax.ShapeDtypeStruct((n_rows // L, 1, WIDTH), F32)],
        scratch_shapes=[pltpu.VMEM((n_seg, seg_rows + 8, CONV_DIM), F32), pltpu.VMEM((R, CONV_DIM), F32),
                        pltpu.VMEM((R, GATE_PAD), F32)],
        compiler_params=_params(("parallel",)),
        name="dn_prep",
    )(qkv, qkv, ab, conv_state, wts["conv_w"], wts["alog"], wts["dtb"], wts["bd_w"], wts["expand"])


def _dn_scan_body(u_ref, w_ref, qe_ref, ks_ref, qk_ref, egl_ref, z_ref, s0_ref, gout_ref, bdw_ref,
                  o_ref, s_out_ref, s_ref, oraw_ref, *, rows, nb, n_steps):
    c = pl.program_id(1)
    L = CHUNK

    @pl.when(c == 0)
    def _():
        s_ref[...] = s0_ref[...]

    r = lax.broadcasted_iota(jnp.int32, (PAIR, PAIR), 0)
    cl = lax.broadcasted_iota(jnp.int32, (PAIR, PAIR), 1)
    same_head = (r // HEAD_DIM) == (cl // HEAD_DIM)
    chains = [(b, p, slice(p * PAIR, (p + 1) * PAIR)) for b in range(nb) for p in range(N_PAIRS)]
    group = 4 * N_PAIRS
    for g0 in range(0, len(chains), group):
        grp = chains[g0:g0 + group]
        ws = [_dot(jnp.concatenate([w_ref[b, :, sl], qe_ref[b, :, sl]], axis=0), s_ref[b, p].astype(BF16))
              for b, p, sl in grp]
        v_new = [(u_ref[b, :, sl] - r[0:L]).astype(BF16) for (b, p, sl), r in zip(grp, ws)]
        for (b, p, sl), r, v in zip(grp, ws, v_new):
            oraw_ref[b, :, sl] = r[L:2 * L] + _dot(qk_ref[b, :, sl], _bdiag(v))
        for (b, p, sl), v in zip(grp, v_new):
            s_ref[b, p] = (s_ref[b, p] * egl_ref[b, 0, :, sl]
                           + jnp.where(same_head, _dot_tn(ks_ref[b, :, sl], v), 0.0))

    o = oraw_ref[...].reshape(nb * L, WIDTH)
    o = o * lax.rsqrt(_head_sumsq(o, bdw_ref[...]) * (1.0 / HEAD_DIM) + NORM_EPS) * gout_ref[...]
    for b in range(nb):
        o_ref[b] = o[b * L:b * L + rows] * _silu(z_ref[b])

    @pl.when(c == n_steps - 1)
    def _():
        s_out_ref[...] = s_ref[...]


def _dn_scan(prep, z, s0_bd, wts, n_streams, seq):
    L = CHUNK
    rows = min(L, seq)
    n_steps = seq // rows
    nb = DN_STREAMS
    u, w, qe, ks, qk, egl = prep
    as3 = lambda a: a.reshape(n_streams, n_steps * L, WIDTH)
    blk = lambda: pl.BlockSpec((nb, L, WIDTH), lambda g, c: (g, c, 0))
    tok = lambda: pl.BlockSpec((nb, rows, WIDTH), lambda g, c: (g, c, 0))
    st = lambda: pl.BlockSpec((nb, N_PAIRS, PAIR, PAIR), lambda g, c: (g, 0, 0, 0))
    body = functools.partial(_dn_scan_body, rows=rows, nb=nb, n_steps=n_steps)
    o, s_new = pl.pallas_call(
        body,
        grid=(n_streams // nb, n_steps),
        in_specs=[blk(), blk(), blk(), blk(), blk(),
                  pl.BlockSpec((nb, 1, 1, WIDTH), lambda g, c: (g, c, 0, 0)), tok(), st(),
                  _const_spec((1, WIDTH)), _const_spec((WIDTH, WIDTH))],
        out_specs=[tok(), st()],
        out_shape=[jax.ShapeDtypeStruct((n_streams, seq, WIDTH), F32),
                   jax.ShapeDtypeStruct((n_streams, N_PAIRS, PAIR, PAIR), F32)],
        scratch_shapes=[pltpu.VMEM((nb, N_PAIRS, PAIR, PAIR), F32), pltpu.VMEM((nb, L, WIDTH), F32)],
        compiler_params=_params(("parallel", "arbitrary")),
        name="dn_scan",
    )(as3(u), as3(w), as3(qe), as3(ks), as3(qk), egl.reshape(n_streams, n_steps, 1, WIDTH),
      z.reshape(n_streams, seq, WIDTH), s0_bd, wts["gout"], wts["bd_w"])
    return o.reshape(n_streams * seq, WIDTH), s_new


def _sb_blocks(probs, tri, valid, keys_minor=False):
    zs = [(_dot if keys_minor else _dot_nt)(q, k) for q, k, _, _ in probs]
    keeps = []
    for z in zs:
        nz = -z
        log_keep = jnp.minimum(nz, 0.0) - jnp.log2(1.0 + jnp.exp2(jnp.minimum(z, nz)))
        if valid is not None:
            log_keep = jnp.where(valid, log_keep, 0.0)
        keeps.append(log_keep.astype(BF16))
    sums = [_dot(lk, tri) for lk in keeps]
    out = []
    for (_, _, v, carry), z, sm in zip(probs, zs, sums):
        if isinstance(carry, int):
            carry = out[carry][1]
        a = jnp.exp2(z + sm)
        if valid is not None:
            a = jnp.where(valid, a, 0.0)
        pv = (_dot_nt if keys_minor else _dot)(a.astype(BF16), v) * jnp.exp2(carry)
        out.append((pv, carry + sm[:, 0:1]))
    return out


def _sb_live(carries):
    top = functools.reduce(jnp.maximum, carries)
    return (jnp.max(top) > SB_DEAD).astype(jnp.int32)


def _sb_tri(n):
    r = lax.broadcasted_iota(jnp.int32, (n, n), 0)
    c = lax.broadcasted_iota(jnp.int32, (n, n), 1)
    return (r >= c).astype(BF16)


def _sb_prompt_body(q_ref, k_ref, v_ref, o_ref):
    i = pl.program_id(1)
    t = SB_BLOCK
    tri = _sb_tri(t)
    r = lax.broadcasted_iota(jnp.int32, (2 * t, t), 0)
    c = lax.broadcasted_iota(jnp.int32, (2 * t, t), 1)
    strict = c < (r & (t - 1))
    head0 = _lane_head0((t, PAIR))
    zero = jnp.zeros((2 * t, 1), F32)
    for p0 in range(0, N_PAIRS, SB_PAIRS):
        sls = [slice(p * PAIR, (p + 1) * PAIR) for p in range(p0, p0 + SB_PAIRS)]
        qs = [_bdiag(q_ref[:, sl]) for sl in sls]

        def probs(j, carries):
            off = pl.multiple_of(j * t, t)
            return [(q, k_ref[pl.ds(off, t), sl], v_ref[pl.ds(off, t), sl], c) for q, sl, c in zip(qs, sls, carries)]

        first = _sb_blocks(probs(i, [zero] * SB_PAIRS), tri, strict)

        def more(st):
            return jnp.logical_and(st[0] < i, st[1] > 0)

        def step(st):
            n, _, cur = st
            res = _sb_blocks(probs(i - 1 - n, [c for _, c in cur]), tri, None)
            new = tuple((acc + pv, c) for (acc, _), (pv, c) in zip(cur, res))
            return n + 1, _sb_live([c for _, c in new]), new

        _, _, last = lax.while_loop(more, step, (jnp.int32(0), _sb_live([c for _, c in first]), tuple(first)))
        for sl, (acc, _) in zip(sls, last):
            o_ref[:, sl] = jnp.where(head0, acc[0:t], acc[t:2 * t])


def _sb_prompt(q16, k16, v16, batch, seq):
    t = SB_BLOCK
    nq = seq // t
    q3 = q16.reshape(batch, seq, WIDTH)
    k3 = k16.reshape(batch, seq, WIDTH)
    v3 = v16.reshape(batch, seq, WIDTH)
    full = pl.BlockSpec((None, seq, WIDTH), lambda b, i: (b, 0, 0))
    blk = pl.BlockSpec((None, t, WIDTH), lambda b, i: (b, i, 0))
    out = pl.pallas_call(
        _sb_prompt_body,
        grid=(batch, nq),
        in_specs=[blk, full, full],
        out_specs=blk,
        out_shape=jax.ShapeDtypeStruct((batch, seq, WIDTH), F32),
        compiler_params=_params(("parallel", "arbitrary")),
        name="sb_prompt",
    )(q3, k3, v3)
    return out.reshape(batch * seq, WIDTH)


def _sb_sample_body(q_ref, kn_ref, vn_ref, kl_ref, vl_ref, kall_ref, vall_ref, o_ref, kbuf, vbuf, sem,
                    *, tq, tk, n_blocks):
    b = pl.program_id(0)
    t = SB_BLOCK
    m = N_HEADS * tq
    tri = _sb_tri(t)
    q = q_ref[...]
    lane = lax.broadcasted_iota(jnp.int32, (tq, WIDTH), 1)
    head_lanes = [(lane >= h * HEAD_DIM) & (lane < (h + 1) * HEAD_DIM) for h in range(N_HEADS)]
    qs = jnp.concatenate([jnp.where(hm, q, jnp.zeros((), BF16)) for hm in head_lanes], axis=0)

    kn = jnp.concatenate([kn_ref[...], jnp.zeros((t - tq, WIDTH), BF16)], axis=0)
    vn = jnp.concatenate([vn_ref[...], jnp.zeros((t - tq, WIDTH), BF16)], axis=0)
    r = lax.broadcasted_iota(jnp.int32, (m, t), 0)
    c = lax.broadcasted_iota(jnp.int32, (m, t), 1)
    (acc, carry), = _sb_blocks([(qs, kn, vn, jnp.zeros((m, 1), F32))], tri, c < (r & (tq - 1)))

    def past(k_ref, v_ref, acc, carry):
        subs = list(reversed(range(tk // t)))
        res = _sb_blocks([(qs, k_ref[:, s * t:(s + 1) * t].astype(BF16), v_ref[:, s * t:(s + 1) * t].astype(BF16),
                           carry if n == 0 else n - 1) for n, s in enumerate(subs)], tri, None, keys_minor=True)
        for pv, _ in res:
            acc = acc + pv
        return acc, res[-1][1]

    acc, carry = past(kl_ref, vl_ref, acc, carry)

    def more(st):
        return jnp.logical_and(st[0] >= 0, st[1] > 0)

    def step(st):
        blk, _, acc, carry = st
        cols = pl.ds(pl.multiple_of(blk * tk, tk), tk)
        copies = [pltpu.make_async_copy(src.at[b, :, cols], dst, sem.at[n])
                  for n, (src, dst) in enumerate(((kall_ref, kbuf), (vall_ref, vbuf)))]
        for cp in copies:
            cp.start()
        for cp in copies:
            cp.wait()
        acc, carry = past(kbuf, vbuf, acc, carry)
        return blk - 1, _sb_live([carry]), acc, carry

    _, _, acc, _ = lax.while_loop(more, step, (jnp.int32(n_blocks - 2), _sb_live([carry]), acc, carry))

    out = jnp.zeros((tq, WIDTH), F32)
    for h, hm in enumerate(head_lanes):
        out = out + jnp.where(hm, acc[h * tq:(h + 1) * tq, :], 0.0)
    o_ref[...] = out


def _sb_sample(q16, k16, v16, past_k, past_v, batch, tq, tk):
    past = past_k.shape[2]
    n_blocks = past // tk
    assert tq & (tq - 1) == 0 and tq <= SB_BLOCK and past % tk == 0
    new = lambda: pl.BlockSpec((None, tq, WIDTH), lambda b: (b, 0, 0))
    latest = lambda: pl.BlockSpec((None, WIDTH, tk), lambda b: (b, 0, n_blocks - 1))
    whole = lambda: pl.BlockSpec(memory_space=pl.ANY)
    body = functools.partial(_sb_sample_body, tq=tq, tk=tk, n_blocks=n_blocks)
    out = pl.pallas_call(
        body,
        grid=(batch,),
        in_specs=[new(), new(), new(), latest(), latest(), whole(), whole()],
        out_specs=new(),
        out_shape=jax.ShapeDtypeStruct((batch, tq, WIDTH), F32),
        scratch_shapes=[pltpu.VMEM((WIDTH, tk), F32), pltpu.VMEM((WIDTH, tk), F32), pltpu.SemaphoreType.DMA((2,))],
        compiler_params=_params(("arbitrary",)),
        name="sb_sample",
    )(q16.reshape(batch, tq, WIDTH), k16.reshape(batch, tq, WIDTH), v16.reshape(batch, tq, WIDTH),
      past_k, past_v, past_k, past_v)
    return out.reshape(batch * tq, WIDTH)


def _post_body(x_ref, oa_ref, ob_ref, p_ref, wo_ref, gm_ref, wu_ref, wd_ref, gp_ref, wg_ref, wp_ref, y_ref):
    def rms(h, g):
        ms = jnp.mean(h * h, axis=-1, keepdims=True)
        return (h * lax.rsqrt(ms + NORM_EPS) * g).astype(BF16)

    y_ref[...] = x_ref[...] + _dot(oa_ref[...].astype(BF16), wo_ref[0:WIDTH, :]) \
        + _dot(ob_ref[...].astype(BF16), wo_ref[WIDTH:2 * WIDTH, :])
    h = y_ref[...]
    u = rms(h, gm_ref[...])
    ff = D_MODEL
    for j in range(D_FF // ff):
        up = jnp.maximum(_dot(u, wu_ref[:, j * ff:(j + 1) * ff]), 0.0)
        h = h + _dot((up * up).astype(BF16), wd_ref[j * ff:(j + 1) * ff, :])
    y_ref[...] = h
    h = y_ref[...]
    gate = _sigmoid(_dot(rms(h, gp_ref[...]), wg_ref[...]))
    y_ref[...] = h + gate * _dot(p_ref[...].astype(BF16), wp_ref[...])


def _post(x2d, oa, ob, p2d, w_out, g_mlp, w_up, w_down, g_ple, w_gate, w_proj, tm):
    n = x2d.shape[0]
    row = lambda w: pl.BlockSpec((tm, w), lambda i: (i, 0))
    cs = lambda a: pl.BlockSpec(a.shape, lambda i: (0, 0), pipeline_mode=pl.Buffered(1))
    return pl.pallas_call(
        _post_body,
        grid=(n // tm,),
        in_specs=[row(D_MODEL), row(WIDTH), row(WIDTH), row(PLE_DIM), cs(w_out), cs(g_mlp), cs(w_up), cs(w_down),
                  cs(g_ple), cs(w_gate), cs(w_proj)],
        out_specs=row(D_MODEL),
        out_shape=jax.ShapeDtypeStruct((n, D_MODEL), F32),
        compiler_params=_params(("parallel",)),
        name="post",
    )(x2d, oa, ob, p2d, w_out, g_mlp, w_up, w_down, g_ple, w_gate, w_proj)


def _pair_blockdiag(s):
    b = s.shape[0]
    s = s.reshape(b, N_PAIRS, 2, HEAD_DIM, HEAD_DIM)
    zero = jnp.zeros_like(s[:, :, 0])
    top = jnp.concatenate([s[:, :, 0], zero], axis=-1)
    bot = jnp.concatenate([zero, s[:, :, 1]], axis=-1)
    return jnp.concatenate([top, bot], axis=-2)


def _pair_unblock(s):
    b = s.shape[0]
    h0 = s[:, :, :HEAD_DIM, :HEAD_DIM]
    h1 = s[:, :, HEAD_DIM:, HEAD_DIM:]
    return jnp.stack([h0, h1], axis=2).reshape(b, N_HEADS, HEAD_DIM, HEAD_DIM)


def _expand_matrix():
    lane = jnp.arange(2 * WIDTH)
    src = (lane // WIDTH) * N_HEADS + (lane % WIDTH) // HEAD_DIM
    return (jnp.arange(GATE_PAD)[:, None] == src[None, :]).astype(BF16)


def _seq_minor(a):
    b, sq, h, d = a.shape
    return a.transpose(0, 2, 3, 1).reshape(b, h * d, sq)


def _group(x, p, conv_state, s0, past_k, past_v, wts, tm, sb_tk):
    batch, seq, _ = x.shape
    n = batch * seq
    x2d = x.reshape(n, D_MODEL)
    qkv, z, ab, q16, k, v, k16, v16 = _inproj(x2d, wts["g_mix"], wts["w_main"], wts["w_ab"], wts["bd_w"],
                                              wts["gq"], wts["gk"], tm, seq)
    prep = _dn_prep(qkv, ab, conv_state, wts, batch, seq)
    o_a, s_new = _dn_scan(prep, z, _pair_blockdiag(s0), wts, batch, seq)
    new_conv = qkv.reshape(batch, seq, CONV_DIM)[:, seq - (CONV_K - 1):, :]
    if past_k is None:
        o_b = _sb_prompt(q16, k16, v16, batch, seq)
    else:
        o_b = _sb_sample(q16, k16, v16, past_k, past_v, batch, seq, sb_tk)
    y = _post(x2d, o_a, o_b, p.reshape(n, PLE_DIM), wts["w_out"], wts["g_mlp"], wts["w_up"], wts["w_down"],
              wts["g_ple"], wts["w_gate"], wts["w_proj"], tm)
    if k.ndim == 3:
        heads = lambda a: a.reshape(batch, N_HEADS, HEAD_DIM, seq).transpose(0, 3, 1, 2)
    else:
        heads = lambda a: a.reshape(batch, seq, N_HEADS, HEAD_DIM)
    return y.reshape(batch, seq, D_MODEL), new_conv, _pair_unblock(s_new), heads(k), heads(v)


def _layer_weights(i, g_mix, w_in, conv_w, a_log, dt_bias, g_out_dn, g_q_sb, g_k_sb, w_out, g_mlp, w_up, w_down,
                   g_ple, w_ple_gate, w_ple_proj):
    w = w_in[i]
    gate0 = CONV_DIM + WIDTH
    sb0 = gate0 + 2 * N_HEADS
    pad_row = lambda a: jnp.pad(a.astype(F32), (0, GATE_PAD - a.shape[0])).reshape(1, GATE_PAD)
    return {
        "g_mix": g_mix[i].reshape(1, D_MODEL),
        "w_main": jnp.concatenate([w[:, :gate0], w[:, sb0:]], axis=1).astype(BF16),
        "w_ab": jnp.pad(w[:, gate0:sb0], ((0, 0), (0, GATE_PAD - 2 * N_HEADS))).astype(BF16),
        "bd_w": _head_blockdiag(WIDTH, HEAD_DIM),
        "expand": _expand_matrix(),
        "gq": jnp.tile(g_q_sb[i], N_HEADS).reshape(1, WIDTH),
        "gk": jnp.tile(g_k_sb[i], N_HEADS).reshape(1, WIDTH),
        "conv_w": conv_w[i],
        "alog": pad_row(a_log[i]),
        "dtb": pad_row(dt_bias[i]),
        "gout": jnp.tile(g_out_dn[i], N_HEADS).reshape(1, WIDTH),
        "w_out": w_out[i].astype(BF16),
        "g_mlp": g_mlp[i].reshape(1, D_MODEL),
        "w_up": w_up[i].astype(BF16),
        "w_down": w_down[i].astype(BF16),
        "g_ple": g_ple[i].reshape(1, D_MODEL),
        "w_gate": w_ple_gate[i].astype(BF16),
        "w_proj": w_ple_proj[i].astype(BF16),
    }


def kernel(x_prompt, x_sample, cache_conv, state_delta, cache_k, cache_v, p_prompt, p_sample, g_mix, w_in, conv_w, a_log, dt_bias, g_out_dn, g_q_sb, g_k_sb, w_out, g_mlp, w_up, w_down, g_ple, w_ple_gate, w_ple_proj):
    depth = w_in.shape[0]
    bp = x_prompt.shape[0]
    y_p, y_s = x_prompt, x_sample
    outs_p, outs_s = [], []
    for i in range(depth):
        wts = _layer_weights(i, g_mix, w_in, conv_w, a_log, dt_bias, g_out_dn, g_q_sb, g_k_sb, w_out, g_mlp,
                             w_up, w_down, g_ple, w_ple_gate, w_ple_proj)
        y_p, *rest = _group(y_p, p_prompt[i], jnp.zeros((bp, CONV_K - 1, CONV_DIM), F32),
                            jnp.zeros((bp, N_HEADS, HEAD_DIM, HEAD_DIM), F32), None, None, wts,
                            tm=512, sb_tk=None)
        outs_p.append(rest)
        dec_b, past_len = cache_k.shape[1], cache_k.shape[2]
        y_s, *rest = _group(y_s, p_sample[i], cache_conv[i], state_delta[i],
                            _seq_minor(cache_k[i]), _seq_minor(cache_v[i]),
                            wts, tm=512, sb_tk=min(2 * SB_BLOCK, past_len))
        outs_s.append(rest)
    stack = lambda outs, j: jnp.stack([o[j] for o in outs])
    return (y_p, y_s,
            stack(outs_p, 0), stack(outs_p, 1), stack(outs_p, 2), stack(outs_p, 3),
            stack(outs_s, 0), stack(outs_s, 1), stack(outs_s, 2), stack(outs_s, 3))
```

```python
import functools
import math

import jax
import jax.numpy as jnp
from jax import lax
from jax.experimental import pallas as pl
from jax.experimental.pallas import tpu as pltpu

F32 = jnp.float32
BF16 = jnp.bfloat16

D_MODEL = 1024
HEAD_DIM = 64
N_HEADS = 8
WIDTH = N_HEADS * HEAD_DIM
N_PAIRS = N_HEADS // 2
PAIR = 2 * HEAD_DIM
CONV_K = 4
CONV_DIM = 3 * WIDTH
CHUNK = 64
D_FF = 4 * D_MODEL
PLE_DIM = 256
SB_SCALE = HEAD_DIM ** -0.5
LOG2E = math.log2(math.e)
NORM_EPS = 1e-6
GATE_PAD = 128
SB_BLOCK = 256
SB_PAIRS = 4
SB_DEAD = -160.0
VMEM_LIMIT = 56 * 1024 * 1024


def _dot(a, b):
    return lax.dot_general(a, b, (((1,), (0,)), ((), ())), preferred_element_type=F32)


def _dot_nt(a, b):
    return lax.dot_general(a, b, (((1,), (1,)), ((), ())), preferred_element_type=F32)


def _dot_tn(a, b):
    return lax.dot_general(a, b, (((0,), (0,)), ((), ())), preferred_element_type=F32)


def _split(x, n):
    parts = []
    r = x
    for i in range(n):
        p = r.astype(BF16)
        parts.append(p)
        if i + 1 < n:
            r = r - p.astype(F32)
    return parts


def _mm(dot, a, b, na, nb):
    ap = [a] if a.dtype == BF16 else _split(a, na)
    bp = [b] if b.dtype == BF16 else _split(b, nb)
    out = None
    for i, x in enumerate(ap):
        for j, y in enumerate(bp):
            if i + j >= max(len(ap), len(bp)):
                continue
            t = dot(x, y)
            out = t if out is None else out + t
    return out


def _head_sumsq(t, bd):
    return _dot((t * t).astype(BF16), bd)


def _softplus(x):
    return jnp.maximum(x, 0.0) + jnp.log1p(jnp.exp(-jnp.abs(x)))


def _sigmoid(x):
    return 1.0 / (1.0 + jnp.exp(-x))


def _silu(x):
    return x * _sigmoid(x)


def _const_spec(shape):
    n = len(shape)
    return pl.BlockSpec(shape, lambda *_: (0,) * n)


def _params(sem):
    return pltpu.CompilerParams(dimension_semantics=sem, vmem_limit_bytes=VMEM_LIMIT)


def _head_blockdiag(n, blk):
    r = jnp.arange(n) // blk
    return (r[:, None] == r[None, :]).astype(BF16)


def _inproj_body(x_ref, g_ref, w_ref, wab_ref, bd_ref, gq_ref, gk_ref,
                 qkv_ref, z_ref, ab_ref, q_ref, k_ref, v_ref, k16_ref, v16_ref, *, seq_minor):
    x = x_ref[...]
    ms = jnp.mean(x * x, axis=-1, keepdims=True)
    u = (x * lax.rsqrt(ms + NORM_EPS) * g_ref[...]).astype(BF16)
    qkv_ref[...] = _dot(u, w_ref[:, 0:CONV_DIM])
    z_ref[...] = _dot(u, w_ref[:, CONV_DIM:CONV_DIM + WIDTH])
    ab_ref[...] = _dot(u, wab_ref[...])
    o = CONV_DIM + WIDTH
    bd = bd_ref[...]

    def headnorm(t, g):
        ss = _head_sumsq(t, bd)
        return t * lax.rsqrt(ss * (1.0 / HEAD_DIM) + NORM_EPS) * g

    q = headnorm(_dot(u, w_ref[:, o:o + WIDTH]), gq_ref[...])
    k = headnorm(_dot(u, w_ref[:, o + WIDTH:o + 2 * WIDTH]), gk_ref[...])
    v = _dot(u, w_ref[:, o + 2 * WIDTH:o + 3 * WIDTH])
    q_ref[...] = (q * (SB_SCALE * LOG2E)).astype(BF16)
    tm = k.shape[0]
    if seq_minor:
        k_ref[...] = k.T
        v_ref[...] = v.T
    else:
        for h in range(N_HEADS):
            k_ref[pl.ds(h, tm, stride=N_HEADS), :] = k[:, h * HEAD_DIM:(h + 1) * HEAD_DIM]
            v_ref[pl.ds(h, tm, stride=N_HEADS), :] = v[:, h * HEAD_DIM:(h + 1) * HEAD_DIM]
    k16_ref[...] = k.astype(BF16)
    v16_ref[...] = v.astype(BF16)


def _inproj(x2d, g_mix, w_main, w_ab, bd, gq, gk, tm, seq):
    n = x2d.shape[0]
    seq_minor = seq % tm == 0
    row = lambda w: pl.BlockSpec((tm, w), lambda i: (i, 0))
    if seq_minor:
        tiles = seq // tm
        kv_spec = pl.BlockSpec((None, WIDTH, tm), lambda i: (i // tiles, 0, i % tiles))
        kv_shape = jax.ShapeDtypeStruct((n // seq, WIDTH, seq), F32)
    else:
        kv_spec = pl.BlockSpec((tm * N_HEADS, HEAD_DIM), lambda i: (i, 0))
        kv_shape = jax.ShapeDtypeStruct((n * N_HEADS, HEAD_DIM), F32)
    outs = [(CONV_DIM, F32), (WIDTH, F32), (GATE_PAD, F32), (WIDTH, BF16), None, None, (WIDTH, BF16), (WIDTH, BF16)]
    return pl.pallas_call(
        functools.partial(_inproj_body, seq_minor=seq_minor),
        grid=(n // tm,),
        in_specs=[row(D_MODEL), _const_spec((1, D_MODEL)), _const_spec(w_main.shape), _const_spec(w_ab.shape),
                  _const_spec(bd.shape), _const_spec((1, WIDTH)), _const_spec((1, WIDTH))],
        out_specs=[kv_spec if o is None else row(o[0]) for o in outs],
        out_shape=[kv_shape if o is None else jax.ShapeDtypeStruct((n, o[0]), o[1]) for o in outs],
        compiler_params=_params(("parallel",)),
        name="inproj",
    )(x2d, g_mix, w_main, w_ab, bd, gq, gk)


DN_CHUNKS = 4
DN_WAVES = 2
DN_STREAMS = 8
DN_PRECISE_FACTORS = 3


def _lane_head0(shape):
    return lax.broadcasted_iota(jnp.int32, shape, len(shape) - 1) % PAIR < HEAD_DIM


def _bdiag(x):
    m0 = _lane_head0(x.shape)
    zero = jnp.zeros((), x.dtype)
    return jnp.concatenate([jnp.where(m0, x, zero), jnp.where(m0, zero, x)], axis=0)


def _dn_prep_body(qkv_ref, halo_ref, ab_ref, cs_ref, cw_ref, alog_ref, dtb_ref, bdw_ref, expand_ref,
                  u_ref, w_ref, qe_ref, ks_ref, qk_ref, egl_ref,
                  xp_ref, y_ref, gb_ref, *, seg_rows, n_seg, n_sub, blocks_per_stream):
    L = CHUNK
    pad = max(seg_rows, L)
    R = n_seg * pad
    nc = R // L
    rows_in = n_seg * seg_rows

    bdw = bdw_ref[...]
    rr = lax.broadcasted_iota(jnp.int32, (R, R), 0)
    cc = lax.broadcasted_iota(jnp.int32, (R, R), 1)
    same = (rr // L) == (cc // L)
    blk_lower = (same & (rr >= cc)).astype(BF16)
    blk_ones = same.astype(BF16)
    colg = lax.broadcasted_iota(jnp.int32, (R, GATE_PAD), 1)
    r512 = lax.broadcasted_iota(jnp.int32, (R, WIDTH), 0)
    l512 = lax.broadcasted_iota(jnp.int32, (R, WIDTH), 1)
    eye = (r512 % L) == (l512 % HEAD_DIM)
    row = lax.broadcasted_iota(jnp.int32, (L, PAIR), 0)
    jcol = lax.broadcasted_iota(jnp.int32, (L, PAIR), 1) % HEAD_DIM
    causal = row >= jcol
    strict = row > jcol

    def prepare(wv):
        o0 = wv * R
        for s in range(n_seg):
            slot = wv * n_seg + s
            r0 = wv * rows_in + s * seg_rows
            if not blocks_per_stream:
                xp_ref[slot, 5:8, :] = cs_ref[slot]
            elif wv > 0:
                xp_ref[slot, 5:8, :] = qkv_ref[r0 - (CONV_K - 1):r0, :]
            else:
                xp_ref[slot, 5:8, :] = halo_ref[5:8, :]

                @pl.when(pl.program_id(0) % blocks_per_stream == 0)
                def _():
                    xp_ref[slot, 5:8, :] = cs_ref[...]
            xp_ref[slot, 8:8 + seg_rows, :] = qkv_ref[r0:r0 + seg_rows, :]
            for c0 in range(0, CONV_DIM, 2 * PAIR):
                cs = slice(c0, c0 + 2 * PAIR)
                y = xp_ref[slot, 5:5 + seg_rows, cs] * cw_ref[0:1, cs]
                for i in range(1, CONV_K):
                    y = y + xp_ref[slot, 5 + i:5 + i + seg_rows, cs] * cw_ref[i:i + 1, cs]
                y_ref[o0 + s * pad:o0 + s * pad + seg_rows, cs] = _silu(y)
                yield
            ab = ab_ref[r0:r0 + seg_rows, :]
            col = lax.broadcasted_iota(jnp.int32, ab.shape, 1)
            g = -jnp.exp(alog_ref[...]) * _softplus(ab + dtb_ref[...])
            gb_ref[o0 + s * pad:o0 + s * pad + seg_rows, :] = jnp.where(col < N_HEADS, g, _sigmoid(ab))
            if seg_rows < pad:
                y_ref[o0 + s * pad + seg_rows:o0 + (s + 1) * pad, :] = jnp.zeros((pad - seg_rows, CONV_DIM), F32)
                gb_ref[o0 + s * pad + seg_rows:o0 + (s + 1) * pad, :] = jnp.zeros((pad - seg_rows, GATE_PAD), F32)

        yield
        q = y_ref[o0:o0 + R, 0:WIDTH]
        q = q * lax.rsqrt(_head_sumsq(q, bdw) + NORM_EPS) * SB_SCALE
        yield
        k = y_ref[o0:o0 + R, WIDTH:2 * WIDTH]
        k = k * lax.rsqrt(_head_sumsq(k, bdw) + NORM_EPS)
        yield
        gb = gb_ref[o0:o0 + R, :]
        gcum = jnp.where(colg < N_HEADS, _mm(_dot, blk_lower, gb, 1, 3), gb)
        ex = _mm(_dot, gcum, expand_ref[...], 3, 1)
        gc_all = ex[:, 0:WIDTH]
        beta_all = ex[:, WIDTH:2 * WIDTH]
        gct_all = _mm(_dot, blk_ones, jnp.where(eye, gc_all, 0.0), 1, 2)

        for c in range(nc):
            egl_ref[wv * nc + c] = jnp.exp(gc_all[(c + 1) * L - 1:(c + 1) * L, :])
        yield
        outs, a_list, x_list = [], [], []
        for c in range(nc):
            for p in range(N_PAIRS):
                rs, sl = slice(c * L, (c + 1) * L), slice(p * PAIR, (p + 1) * PAIR)
                ro = slice(o0 + c * L, o0 + (c + 1) * L)
                qc = q[rs, sl]
                kc = k[rs, sl]
                vc = y_ref[ro, 2 * WIDTH + sl.start:2 * WIDTH + sl.stop]
                gc = gc_all[rs, sl]
                beta = beta_all[rs, sl]
                eg = jnp.exp(gc)
                decay = jnp.where(causal, jnp.exp(jnp.where(causal, gc - gct_all[rs, sl], 0.0)), 0.0)
                kb = kc * beta
                kst = _bdiag(kc.astype(BF16))
                aq = _dot_nt(jnp.concatenate([kb, qc], axis=0).astype(BF16), kst)
                outs.append((ro, sl))
                a_list.append(jnp.where(strict, aq[0:L] * decay, 0.0))
                x_list.append(jnp.concatenate([vc * beta, kb * eg], axis=1))
                qk_ref[ro, sl] = jnp.where(causal, aq[L:2 * L] * decay, 0.0).astype(BF16)
                qe_ref[ro, sl] = (qc * eg).astype(BF16)
                ks_ref[ro, sl] = (kc * jnp.exp(gc[L - 1:L, :] - gc)).astype(BF16)
                yield
        return outs, a_list, x_list

    def factor(t, a_list, x_list, between):
        pieces = 2 if t < DN_PRECISE_FACTORS else 1
        for n in range(len(a_list)):
            a_parts, x_parts = _split(a_list[n], pieces), _split(x_list[n], pieces)
            if t < 5:
                w_parts = [jnp.concatenate([_bdiag(ap), _bdiag(xp)], axis=1) for ap, xp in zip(a_parts, x_parts)]
            else:
                w_parts = [_bdiag(xp) for xp in x_parts]
            r = _dot(a_parts[0], w_parts[0])
            if pieces == 2:
                r = r + _dot(a_parts[0], w_parts[1])
            if t < 5:
                a_list[n], ax = r[:, 0:PAIR], r[:, PAIR:]
            else:
                ax = r
            x_list[n] = x_list[n] - ax if t == 0 else x_list[n] + ax
            if n % 2:
                between()

    def advance(gen, box):
        if not box:
            try:
                next(gen)
            except StopIteration as stop:
                box.append(stop.value)

    gen, box = prepare(0), []
    while not box:
        advance(gen, box)
    for wv in range(n_sub):
        outs, a_list, x_list = box[0]
        gen, box = (prepare(wv + 1), []) if wv + 1 < n_sub else (None, [None])
        for t in range(6):
            factor(t, a_list, x_list, functools.partial(advance, gen, box))
        while not box:
            advance(gen, box)
        for (ro, sl), x in zip(outs, x_list):
            u_ref[ro, sl] = x[:, 0:PAIR]
            w_ref[ro, sl] = x[:, PAIR:].astype(BF16)


def _dn_prep(qkv, ab, conv_state, wts, n_streams, seq):
    L = CHUNK
    n_sub = DN_WAVES
    if seq >= L:
        seg_rows, n_seg = DN_CHUNKS * L, 1
        blocks_per_stream = seq // (n_sub * seg_rows)
        cs_spec = pl.BlockSpec((None, CONV_K - 1, CONV_DIM), lambda i: (i // blocks_per_stream, 0, 0))
    else:
        seg_rows, n_seg, blocks_per_stream = seq, DN_CHUNKS, 0
        cs_spec = pl.BlockSpec((n_sub * n_seg, CONV_K - 1, CONV_DIM), lambda i: (i, 0, 0))
    rows_in = n_sub * n_seg * seg_rows
    R = n_sub * n_seg * max(seg_rows, L)
    n_blocks = n_streams * seq // rows_in
    tok = lambda w: pl.BlockSpec((rows_in, w), lambda i: (i, 0))
    halo = pl.BlockSpec((8, CONV_DIM), lambda i: (jnp.maximum(i * (rows_in // 8) - 1, 0), 0))
    out = lambda: pl.BlockSpec((R, WIDTH), lambda i: (i, 0))
    n_rows = n_blocks * R
    body = functools.partial(_dn_prep_body, seg_rows=seg_rows, n_seg=n_seg, n_sub=n_sub,
                             blocks_per_stream=blocks_per_stream)
    return pl.pallas_call(
        body,
        grid=(n_blocks,),
        in_specs=[tok(CONV_DIM), halo, tok(GATE_PAD), cs_spec,
                  _const_spec((CONV_K, CONV_DIM)), _const_spec((1, GATE_PAD)), _const_spec((1, GATE_PAD)),
                  _const_spec((WIDTH, WIDTH)), _const_spec((GATE_PAD, 2 * WIDTH))],
        out_specs=[out(), out(), out(), out(), out(), pl.BlockSpec((R // L, 1, WIDTH), lambda i: (i, 0, 0))],
        out_shape=[jax.ShapeDtypeStruct((n_rows, WIDTH), F32)] +
                  [jax.ShapeDtypeStruct((n_rows, WIDTH), BF16)] * 4 +
                  [jax.ShapeDtypeStruct((n_rows // L, 1, WIDTH), F32)],
        scratch_shapes=[pltpu.VMEM((n_sub * n_seg, seg_rows + 8, CONV_DIM), F32), pltpu.VMEM((R, CONV_DIM), F32),
                        pltpu.VMEM((R, GATE_PAD), F32)],
        compiler_params=_params(("parallel",)),
        name="dn_prep",
    )(qkv, qkv, ab, conv_state, wts["conv_w"], wts["alog"], wts["dtb"], wts["bd_w"], wts["expand"])


def _dn_scan_body(u_ref, w_ref, qe_ref, ks_ref, qk_ref, egl_ref, z_ref, s0_ref, gout_ref, bdw_ref,
                  o_ref, s_out_ref, s_ref, oraw_ref, *, rows, nb, n_steps):
    c = pl.program_id(1)
    L = CHUNK

    @pl.when(c == 0)
    def _():
        s_ref[...] = s0_ref[...]

    r = lax.broadcasted_iota(jnp.int32, (PAIR, PAIR), 0)
    cl = lax.broadcasted_iota(jnp.int32, (PAIR, PAIR), 1)
    same_head = (r // HEAD_DIM) == (cl // HEAD_DIM)
    chains = [(b, p, slice(p * PAIR, (p + 1) * PAIR)) for b in range(nb) for p in range(N_PAIRS)]
    group = 4 * N_PAIRS
    for g0 in range(0, len(chains), group):
        grp = chains[g0:g0 + group]
        ws = [_dot(jnp.concatenate([w_ref[b, :, sl], qe_ref[b, :, sl]], axis=0), s_ref[b, p].astype(BF16))
              for b, p, sl in grp]
        v_new = [(u_ref[b, :, sl] - r[0:L]).astype(BF16) for (b, p, sl), r in zip(grp, ws)]
        for (b, p, sl), r, v in zip(grp, ws, v_new):
            oraw_ref[b, :, sl] = r[L:2 * L] + _dot(qk_ref[b, :, sl], _bdiag(v))
        for (b, p, sl), v in zip(grp, v_new):
            s_ref[b, p] = (s_ref[b, p] * egl_ref[b, 0, :, sl]
                           + jnp.where(same_head, _dot_tn(ks_ref[b, :, sl], v), 0.0))

    o = oraw_ref[...].reshape(nb * L, WIDTH)
    o = o * lax.rsqrt(_head_sumsq(o, bdw_ref[...]) * (1.0 / HEAD_DIM) + NORM_EPS) * gout_ref[...]
    for b in range(nb):
        o_ref[b] = o[b * L:b * L + rows] * _silu(z_ref[b])

    @pl.when(c == n_steps - 1)
    def _():
        s_out_ref[...] = s_ref[...]


def _dn_scan(prep, z, s0_bd, wts, n_streams, seq):
    L = CHUNK
    rows = min(L, seq)
    n_steps = seq // rows
    nb = DN_STREAMS
    u, w, qe, ks, qk, egl = prep
    as3 = lambda a: a.reshape(n_streams, n_steps * L, WIDTH)
    blk = lambda: pl.BlockSpec((nb, L, WIDTH), lambda g, c: (g, c, 0))
    tok = lambda: pl.BlockSpec((nb, rows, WIDTH), lambda g, c: (g, c, 0))
    st = lambda: pl.BlockSpec((nb, N_PAIRS, PAIR, PAIR), lambda g, c: (g, 0, 0, 0))
    body = functools.partial(_dn_scan_body, rows=rows, nb=nb, n_steps=n_steps)
    o, s_new = pl.pallas_call(
        body,
        grid=(n_streams // nb, n_steps),
        in_specs=[blk(), blk(), blk(), blk(), blk(),
                  pl.BlockSpec((nb, 1, 1, WIDTH), lambda g, c: (g, c, 0, 0)), tok(), st(),
                  _const_spec((1, WIDTH)), _const_spec((WIDTH, WIDTH))],
        out_specs=[tok(), st()],
        out_shape=[jax.ShapeDtypeStruct((n_streams, seq, WIDTH), F32),
                   jax.ShapeDtypeStruct((n_streams, N_PAIRS, PAIR, PAIR), F32)],
        scratch_shapes=[pltpu.VMEM((nb, N_PAIRS, PAIR, PAIR), F32), pltpu.VMEM((nb, L, WIDTH), F32)],
        compiler_params=_params(("parallel", "arbitrary")),
        name="dn_scan",
    )(as3(u), as3(w), as3(qe), as3(ks), as3(qk), egl.reshape(n_streams, n_steps, 1, WIDTH),
      z.reshape(n_streams, seq, WIDTH), s0_bd, wts["gout"], wts["bd_w"])
    return o.reshape(n_streams * seq, WIDTH), s_new


def _sb_blocks(probs, tri, valid, keys_minor=False):
    zs = [(_dot if keys_minor else _dot_nt)(q, k) for q, k, _, _ in probs]
    keeps = []
    for z in zs:
        nz = -z
        log_keep = jnp.minimum(nz, 0.0) - jnp.log2(1.0 + jnp.exp2(jnp.minimum(z, nz)))
        if valid is not None:
            log_keep = jnp.where(valid, log_keep, 0.0)
        keeps.append(log_keep.astype(BF16))
    sums = [_dot(lk, tri) for lk in keeps]
    out = []
    for (_, _, v, carry), z, sm in zip(probs, zs, sums):
        if isinstance(carry, int):
            carry = out[carry][1]
        a = jnp.exp2(z + sm)
        if valid is not None:
            a = jnp.where(valid, a, 0.0)
        pv = (_dot_nt if keys_minor else _dot)(a.astype(BF16), v) * jnp.exp2(carry)
        out.append((pv, carry + sm[:, 0:1]))
    return out


def _sb_live(carries):
    top = functools.reduce(jnp.maximum, carries)
    return (jnp.max(top) > SB_DEAD).astype(jnp.int32)


def _sb_tri(n):
    r = lax.broadcasted_iota(jnp.int32, (n, n), 0)
    c = lax.broadcasted_iota(jnp.int32, (n, n), 1)
    return (r >= c).astype(BF16)


def _sb_prompt_body(q_ref, k_ref, v_ref, o_ref):
    i = pl.program_id(1)
    t = SB_BLOCK
    tri = _sb_tri(t)
    r = lax.broadcasted_iota(jnp.int32, (2 * t, t), 0)
    c = lax.broadcasted_iota(jnp.int32, (2 * t, t), 1)
    strict = c < (r & (t - 1))
    head0 = _lane_head0((t, PAIR))
    zero = jnp.zeros((2 * t, 1), F32)
    for p0 in range(0, N_PAIRS, SB_PAIRS):
        sls = [slice(p * PAIR, (p + 1) * PAIR) for p in range(p0, p0 + SB_PAIRS)]
        qs = [_bdiag(q_ref[:, sl]) for sl in sls]

        def probs(j, carries):
            off = pl.multiple_of(j * t, t)
            return [(q, k_ref[pl.ds(off, t), sl], v_ref[pl.ds(off, t), sl], c) for q, sl, c in zip(qs, sls, carries)]

        first = _sb_blocks(probs(i, [zero] * SB_PAIRS), tri, strict)

        def more(st):
            return jnp.logical_and(st[0] < i, st[1] > 0)

        def step(st):
            n, _, cur = st
            res = _sb_blocks(probs(i - 1 - n, [c for _, c in cur]), tri, None)
            new = tuple((acc + pv, c) for (acc, _), (pv, c) in zip(cur, res))
            return n + 1, _sb_live([c for _, c in new]), new

        _, _, last = lax.while_loop(more, step, (jnp.int32(0), _sb_live([c for _, c in first]), tuple(first)))
        for sl, (acc, _) in zip(sls, last):
            o_ref[:, sl] = jnp.where(head0, acc[0:t], acc[t:2 * t])


def _sb_prompt(q16, k16, v16, batch, seq):
    t = SB_BLOCK
    nq = seq // t
    q3 = q16.reshape(batch, seq, WIDTH)
    k3 = k16.reshape(batch, seq, WIDTH)
    v3 = v16.reshape(batch, seq, WIDTH)
    full = pl.BlockSpec((None, seq, WIDTH), lambda b, i: (b, 0, 0))
    blk = pl.BlockSpec((None, t, WIDTH), lambda b, i: (b, i, 0))
    out = pl.pallas_call(
        _sb_prompt_body,
        grid=(batch, nq),
        in_specs=[blk, full, full],
        out_specs=blk,
        out_shape=jax.ShapeDtypeStruct((batch, seq, WIDTH), F32),
        compiler_params=_params(("parallel", "arbitrary")),
        name="sb_prompt",
    )(q3, k3, v3)
    return out.reshape(batch * seq, WIDTH)


def _sb_sample_body(q_ref, kn_ref, vn_ref, kl_ref, vl_ref, kall_ref, vall_ref, o_ref, kbuf, vbuf, sem,
                    *, tq, tk, n_blocks):
    b = pl.program_id(0)
    t = SB_BLOCK
    m = N_HEADS * tq
    tri = _sb_tri(t)
    q = q_ref[...]
    lane = lax.broadcasted_iota(jnp.int32, (tq, WIDTH), 1)
    head_lanes = [(lane >= h * HEAD_DIM) & (lane < (h + 1) * HEAD_DIM) for h in range(N_HEADS)]
    qs = jnp.concatenate([jnp.where(hm, q, jnp.zeros((), BF16)) for hm in head_lanes], axis=0)

    kn = jnp.concatenate([kn_ref[...], jnp.zeros((t - tq, WIDTH), BF16)], axis=0)
    vn = jnp.concatenate([vn_ref[...], jnp.zeros((t - tq, WIDTH), BF16)], axis=0)
    r = lax.broadcasted_iota(jnp.int32, (m, t), 0)
    c = lax.broadcasted_iota(jnp.int32, (m, t), 1)
    (acc, carry), = _sb_blocks([(qs, kn, vn, jnp.zeros((m, 1), F32))], tri, c < (r & (tq - 1)))

    def past(k_ref, v_ref, acc, carry):
        subs = list(reversed(range(tk // t)))
        res = _sb_blocks([(qs, k_ref[:, s * t:(s + 1) * t].astype(BF16), v_ref[:, s * t:(s + 1) * t].astype(BF16),
                           carry if n == 0 else n - 1) for n, s in enumerate(subs)], tri, None, keys_minor=True)
        for pv, _ in res:
            acc = acc + pv
        return acc, res[-1][1]

    acc, carry = past(kl_ref, vl_ref, acc, carry)

    def more(st):
        return jnp.logical_and(st[0] >= 0, st[1] > 0)

    def step(st):
        blk, _, acc, carry = st
        cols = pl.ds(pl.multiple_of(blk * tk, tk), tk)
        copies = [pltpu.make_async_copy(src.at[b, :, cols], dst, sem.at[n])
                  for n, (src, dst) in enumerate(((kall_ref, kbuf), (vall_ref, vbuf)))]
        for cp in copies:
            cp.start()
        for cp in copies:
            cp.wait()
        acc, carry = past(kbuf, vbuf, acc, carry)
        return blk - 1, _sb_live([carry]), acc, carry

    _, _, acc, _ = lax.while_loop(more, step, (jnp.int32(n_blocks - 2), _sb_live([carry]), acc, carry))

    out = jnp.zeros((tq, WIDTH), F32)
    for h, hm in enumerate(head_lanes):
        out = out + jnp.where(hm, acc[h * tq:(h + 1) * tq, :], 0.0)
    o_ref[...] = out


def _sb_sample(q16, k16, v16, past_k, past_v, batch, tq, tk):
    past = past_k.shape[2]
    n_blocks = past // tk
    assert tq & (tq - 1) == 0 and tq <= SB_BLOCK and past % tk == 0
    new = lambda: pl.BlockSpec((None, tq, WIDTH), lambda b: (b, 0, 0))
    latest = lambda: pl.BlockSpec((None, WIDTH, tk), lambda b: (b, 0, n_blocks - 1))
    whole = lambda: pl.BlockSpec(memory_space=pl.ANY)
    body = functools.partial(_sb_sample_body, tq=tq, tk=tk, n_blocks=n_blocks)
    out = pl.pallas_call(
        body,
        grid=(batch,),
        in_specs=[new(), new(), new(), latest(), latest(), whole(), whole()],
        out_specs=new(),
        out_shape=jax.ShapeDtypeStruct((batch, tq, WIDTH), F32),
        scratch_shapes=[pltpu.VMEM((WIDTH, tk), F32), pltpu.VMEM((WIDTH, tk), F32), pltpu.SemaphoreType.DMA((2,))],
        compiler_params=_params(("arbitrary",)),
        name="sb_sample",
    )(q16.reshape(batch, tq, WIDTH), k16.reshape(batch, tq, WIDTH), v16.reshape(batch, tq, WIDTH),
      past_k, past_v, past_k, past_v)
    return out.reshape(batch * tq, WIDTH)


def _post_body(x_ref, oa_ref, ob_ref, p_ref, wo_ref, gm_ref, wu_ref, wd_ref, gp_ref, wg_ref, wp_ref, y_ref):
    def rms(h, g):
        ms = jnp.mean(h * h, axis=-1, keepdims=True)
        return (h * lax.rsqrt(ms + NORM_EPS) * g).astype(BF16)

    y_ref[...] = x_ref[...] + _dot(oa_ref[...].astype(BF16), wo_ref[0:WIDTH, :]) \
        + _dot(ob_ref[...].astype(BF16), wo_ref[WIDTH:2 * WIDTH, :])
    h = y_ref[...]
    u = rms(h, gm_ref[...])
    ff = D_MODEL
    for j in range(D_FF // ff):
        up = jnp.maximum(_dot(u, wu_ref[:, j * ff:(j + 1) * ff]), 0.0)
        h = h + _dot((up * up).astype(BF16), wd_ref[j * ff:(j + 1) * ff, :])
    y_ref[...] = h
    h = y_ref[...]
    gate = _sigmoid(_dot(rms(h, gp_ref[...]), wg_ref[...]))
    y_ref[...] = h + gate * _dot(p_ref[...].astype(BF16), wp_ref[...])


def _post(x2d, oa, ob, p2d, w_out, g_mlp, w_up, w_down, g_ple, w_gate, w_proj, tm):
    n = x2d.shape[0]
    row = lambda w: pl.BlockSpec((tm, w), lambda i: (i, 0))
    cs = lambda a: pl.BlockSpec(a.shape, lambda i: (0, 0), pipeline_mode=pl.Buffered(1))
    return pl.pallas_call(
        _post_body,
        grid=(n // tm,),
        in_specs=[row(D_MODEL), row(WIDTH), row(WIDTH), row(PLE_DIM), cs(w_out), cs(g_mlp), cs(w_up), cs(w_down),
                  cs(g_ple), cs(w_gate), cs(w_proj)],
        out_specs=row(D_MODEL),
        out_shape=jax.ShapeDtypeStruct((n, D_MODEL), F32),
        compiler_params=_params(("parallel",)),
        name="post",
    )(x2d, oa, ob, p2d, w_out, g_mlp, w_up, w_down, g_ple, w_gate, w_proj)


def _pair_blockdiag(s):
    b = s.shape[0]
    s = s.reshape(b, N_PAIRS, 2, HEAD_DIM, HEAD_DIM)
    zero = jnp.zeros_like(s[:, :, 0])
    top = jnp.concatenate([s[:, :, 0], zero], axis=-1)
    bot = jnp.concatenate([zero, s[:, :, 1]], axis=-1)
    return jnp.concatenate([top, bot], axis=-2)


def _pair_unblock(s):
    b = s.shape[0]
    h0 = s[:, :, :HEAD_DIM, :HEAD_DIM]
    h1 = s[:, :, HEAD_DIM:, HEAD_DIM:]
    return jnp.stack([h0, h1], axis=2).reshape(b, N_HEADS, HEAD_DIM, HEAD_DIM)


def _expand_matrix():
    lane = jnp.arange(2 * WIDTH)
    src = (lane // WIDTH) * N_HEADS + (lane % WIDTH) // HEAD_DIM
    return (jnp.arange(GATE_PAD)[:, None] == src[None, :]).astype(BF16)


def _seq_minor(a):
    b, sq, h, d = a.shape
    return a.transpose(0, 2, 3, 1).reshape(b, h * d, sq)


def _group(x, p, conv_state, s0, past_k, past_v, wts, tm, sb_tk):
    batch, seq, _ = x.shape
    n = batch * seq
    x2d = x.reshape(n, D_MODEL)
    qkv, z, ab, q16, k, v, k16, v16 = _inproj(x2d, wts["g_mix"], wts["w_main"], wts["w_ab"], wts["bd_w"],
                                              wts["gq"], wts["gk"], tm, seq)
    prep = _dn_prep(qkv, ab, conv_state, wts, batch, seq)
    o_a, s_new = _dn_scan(prep, z, _pair_blockdiag(s0), wts, batch, seq)
    new_conv = qkv.reshape(batch, seq, CONV_DIM)[:, seq - (CONV_K - 1):, :]
    if past_k is None:
        o_b = _sb_prompt(q16, k16, v16, batch, seq)
    else:
        o_b = _sb_sample(q16, k16, v16, past_k, past_v, batch, seq, sb_tk)
    y = _post(x2d, o_a, o_b, p.reshape(n, PLE_DIM), wts["w_out"], wts["g_mlp"], wts["w_up"], wts["w_down"],
              wts["g_ple"], wts["w_gate"], wts["w_proj"], tm)
    if k.ndim == 3:
        heads = lambda a: a.reshape(batch, N_HEADS, HEAD_DIM, seq).transpose(0, 3, 1, 2)
    else:
        heads = lambda a: a.reshape(batch, seq, N_HEADS, HEAD_DIM)
    return y.reshape(batch, seq, D_MODEL), new_conv, _pair_unblock(s_new), heads(k), heads(v)


def _layer_weights(i, g_mix, w_in, conv_w, a_log, dt_bias, g_out_dn, g_q_sb, g_k_sb, w_out, g_mlp, w_up, w_down,
                   g_ple, w_ple_gate, w_ple_proj):
    w = w_in[i]
    gate0 = CONV_DIM + WIDTH
    sb0 = gate0 + 2 * N_HEADS
    pad_row = lambda a: jnp.pad(a.astype(F32), (0, GATE_PAD - a.shape[0])).reshape(1, GATE_PAD)
    return {
        "g_mix": g_mix[i].reshape(1, D_MODEL),
        "w_main": jnp.concatenate([w[:, :gate0], w[:, sb0:]], axis=1).astype(BF16),
        "w_ab": jnp.pad(w[:, gate0:sb0], ((0, 0), (0, GATE_PAD - 2 * N_HEADS))).astype(BF16),
        "bd_w": _head_blockdiag(WIDTH, HEAD_DIM),
        "expand": _expand_matrix(),
        "gq": jnp.tile(g_q_sb[i], N_HEADS).reshape(1, WIDTH),
        "gk": jnp.tile(g_k_sb[i], N_HEADS).reshape(1, WIDTH),
        "conv_w": conv_w[i],
        "alog": pad_row(a_log[i]),
        "dtb": pad_row(dt_bias[i]),
        "gout": jnp.tile(g_out_dn[i], N_HEADS).reshape(1, WIDTH),
        "w_out": w_out[i].astype(BF16),
        "g_mlp": g_mlp[i].reshape(1, D_MODEL),
        "w_up": w_up[i].astype(BF16),
        "w_down": w_down[i].astype(BF16),
        "g_ple": g_ple[i].reshape(1, D_MODEL),
        "w_gate": w_ple_gate[i].astype(BF16),
        "w_proj": w_ple_proj[i].astype(BF16),
    }


def kernel(x_prompt, x_sample, cache_conv, state_delta, cache_k, cache_v, p_prompt, p_sample, g_mix, w_in, conv_w, a_log, dt_bias, g_out_dn, g_q_sb, g_k_sb, w_out, g_mlp, w_up, w_down, g_ple, w_ple_gate, w_ple_proj):
    depth = w_in.shape[0]
    bp = x_prompt.shape[0]
    y_p, y_s = x_prompt, x_sample
    outs_p, outs_s = [], []
    for i in range(depth):
        wts = _layer_weights(i, g_mix, w_in, conv_w, a_log, dt_bias, g_out_dn, g_q_sb, g_k_sb, w_out, g_mlp,
                             w_up, w_down, g_ple, w_ple_gate, w_ple_proj)
        y_p, *rest = _group(y_p, p_prompt[i], jnp.zeros((bp, CONV_K - 1, CONV_DIM), F32),
                            jnp.zeros((bp, N_HEADS, HEAD_DIM, HEAD_DIM), F32), None, None, wts,
                            tm=512, sb_tk=None)
        outs_p.append(rest)
        dec_b, past_len = cache_k.shape[1], cache_k.shape[2]
        y_s, *rest = _group(y_s, p_sample[i], cache_conv[i], state_delta[i],
                            _seq_minor(cache_k[i]), _seq_minor(cache_v[i]),
                            wts, tm=512, sb_tk=min(2 * SB_BLOCK, past_len))
        outs_s.append(rest)
    stack = lambda outs, j: jnp.stack([o[j] for o in outs])
    return (y_p, y_s,
            stack(outs_p, 0), stack(outs_p, 1), stack(outs_p, 2), stack(outs_p, 3),
            stack(outs_s, 0), stack(outs_s, 1), stack(outs_s, 2), stack(outs_s, 3))
```

```python
import functools
import math

import jax
import jax.numpy as jnp
from jax import lax
from jax.experimental import pallas as pl
from jax.experimental.pallas import tpu as pltpu

F32 = jnp.float32
BF16 = jnp.bfloat16

D_MODEL = 1024
HEAD_DIM = 64
N_HEADS = 8
WIDTH = N_HEADS * HEAD_DIM
N_PAIRS = N_HEADS // 2
PAIR = 2 * HEAD_DIM
CONV_K = 4
CONV_DIM = 3 * WIDTH
CHUNK = 64
D_FF = 4 * D_MODEL
PLE_DIM = 256
SB_SCALE = HEAD_DIM ** -0.5
LOG2E = math.log2(math.e)
NORM_EPS = 1e-6
GATE_PAD = 128
SB_BLOCK = 256
SB_PAIRS = 4
SB_DEAD = -160.0
VMEM_LIMIT = 56 * 1024 * 1024


def _dot(a, b):
    return lax.dot_general(a, b, (((1,), (0,)), ((), ())), preferred_element_type=F32)


def _dot_nt(a, b):
    return lax.dot_general(a, b, (((1,), (1,)), ((), ())), preferred_element_type=F32)


def _dot_tn(a, b):
    return lax.dot_general(a, b, (((0,), (0,)), ((), ())), preferred_element_type=F32)


def _split(x, n):
    parts = []
    r = x
    for i in range(n):
        p = r.astype(BF16)
        parts.append(p)
        if i + 1 < n:
            r = r - p.astype(F32)
    return parts


def _mm(dot, a, b, na, nb):
    ap = [a] if a.dtype == BF16 else _split(a, na)
    bp = [b] if b.dtype == BF16 else _split(b, nb)
    out = None
    for i, x in enumerate(ap):
        for j, y in enumerate(bp):
            if i + j >= max(len(ap), len(bp)):
                continue
            t = dot(x, y)
            out = t if out is None else out + t
    return out


def _head_sumsq(t, bd):
    return _dot((t * t).astype(BF16), bd)


def _softplus(x):
    return jnp.maximum(x, 0.0) + jnp.log1p(jnp.exp(-jnp.abs(x)))


def _sigmoid(x):
    return 1.0 / (1.0 + jnp.exp(-x))


def _silu(x):
    return x * _sigmoid(x)


def _const_spec(shape):
    n = len(shape)
    return pl.BlockSpec(shape, lambda *_: (0,) * n)


def _params(sem):
    return pltpu.CompilerParams(dimension_semantics=sem, vmem_limit_bytes=VMEM_LIMIT)


def _head_blockdiag(n, blk):
    r = jnp.arange(n) // blk
    return (r[:, None] == r[None, :]).astype(BF16)


def _inproj_body(x_ref, g_ref, w_ref, wab_ref, bd_ref, gq_ref, gk_ref,
                 qkv_ref, z_ref, ab_ref, q_ref, k_ref, v_ref, k16_ref, v16_ref, *, seq_minor):
    x = x_ref[...]
    ms = jnp.mean(x * x, axis=-1, keepdims=True)
    u = (x * lax.rsqrt(ms + NORM_EPS) * g_ref[...]).astype(BF16)
    qkv_ref[...] = _dot(u, w_ref[:, 0:CONV_DIM])
    z_ref[...] = _dot(u, w_ref[:, CONV_DIM:CONV_DIM + WIDTH])
    ab_ref[...] = _dot(u, wab_ref[...])
    o = CONV_DIM + WIDTH
    bd = bd_ref[...]

    def headnorm(t, g):
        ss = _head_sumsq(t, bd)
        return t * lax.rsqrt(ss * (1.0 / HEAD_DIM) + NORM_EPS) * g

    q = headnorm(_dot(u, w_ref[:, o:o + WIDTH]), gq_ref[...])
    k = headnorm(_dot(u, w_ref[:, o + WIDTH:o + 2 * WIDTH]), gk_ref[...])
    v = _dot(u, w_ref[:, o + 2 * WIDTH:o + 3 * WIDTH])
    q_ref[...] = (q * (SB_SCALE * LOG2E)).astype(BF16)
    tm = k.shape[0]
    if seq_minor:
        k_ref[...] = k.T
        v_ref[...] = v.T
    else:
        for h in range(N_HEADS):
            k_ref[pl.ds(h, tm, stride=N_HEADS), :] = k[:, h * HEAD_DIM:(h + 1) * HEAD_DIM]
            v_ref[pl.ds(h, tm, stride=N_HEADS), :] = v[:, h * HEAD_DIM:(h + 1) * HEAD_DIM]
    k16_ref[...] = k.astype(BF16)
    v16_ref[...] = v.astype(BF16)


def _inproj(x2d, g_mix, w_main, w_ab, bd, gq, gk, tm, seq):
    n = x2d.shape[0]
    seq_minor = seq % tm == 0
    row = lambda w: pl.BlockSpec((tm, w), lambda i: (i, 0))
    if seq_minor:
        tiles = seq // tm
        kv_spec = pl.BlockSpec((None, WIDTH, tm), lambda i: (i // tiles, 0, i % tiles))
        kv_shape = jax.ShapeDtypeStruct((n // seq, WIDTH, seq), F32)
    else:
        kv_spec = pl.BlockSpec((tm * N_HEADS, HEAD_DIM), lambda i: (i, 0))
        kv_shape = jax.ShapeDtypeStruct((n * N_HEADS, HEAD_DIM), F32)
    outs = [(CONV_DIM, F32), (WIDTH, F32), (GATE_PAD, F32), (WIDTH, BF16), None, None, (WIDTH, BF16), (WIDTH, BF16)]
    return pl.pallas_call(
        functools.partial(_inproj_body, seq_minor=seq_minor),
        grid=(n // tm,),
        in_specs=[row(D_MODEL), _const_spec((1, D_MODEL)), _const_spec(w_main.shape), _const_spec(w_ab.shape),
                  _const_spec(bd.shape), _const_spec((1, WIDTH)), _const_spec((1, WIDTH))],
        out_specs=[kv_spec if o is None else row(o[0]) for o in outs],
        out_shape=[kv_shape if o is None else jax.ShapeDtypeStruct((n, o[0]), o[1]) for o in outs],
        compiler_params=_params(("parallel",)),
        name="inproj",
    )(x2d, g_mix, w_main, w_ab, bd, gq, gk)


DN_CHUNKS = 4
DN_WAVES = 2
DN_STREAMS = 8
DN_PRECISE_FACTORS = 3


def _lane_head0(shape):
    return lax.broadcasted_iota(jnp.int32, shape, len(shape) - 1) % PAIR < HEAD_DIM


def _bdiag(x):
    m0 = _lane_head0(x.shape)
    zero = jnp.zeros((), x.dtype)
    return jnp.concatenate([jnp.where(m0, x, zero), jnp.where(m0, zero, x)], axis=0)


def _dn_prep_body(qkv_ref, halo_ref, ab_ref, cs_ref, cw_ref, alog_ref, dtb_ref, bdw_ref, expand_ref,
                  u_ref, w_ref, qe_ref, ks_ref, qk_ref, egl_ref,
                  xp_ref, y_ref, gb_ref, *, seg_rows, n_seg, n_sub, blocks_per_stream):
    L = CHUNK
    pad = max(seg_rows, L)
    R = n_seg * pad
    nc = R // L
    rows_in = n_seg * seg_rows

    bdw = bdw_ref[...]
    rr = lax.broadcasted_iota(jnp.int32, (R, R), 0)
    cc = lax.broadcasted_iota(jnp.int32, (R, R), 1)
    same = (rr // L) == (cc // L)
    blk_lower = (same & (rr >= cc)).astype(BF16)
    blk_ones = same.astype(BF16)
    colg = lax.broadcasted_iota(jnp.int32, (R, GATE_PAD), 1)
    r512 = lax.broadcasted_iota(jnp.int32, (R, WIDTH), 0)
    l512 = lax.broadcasted_iota(jnp.int32, (R, WIDTH), 1)
    eye = (r512 % L) == (l512 % HEAD_DIM)
    row = lax.broadcasted_iota(jnp.int32, (L, PAIR), 0)
    jcol = lax.broadcasted_iota(jnp.int32, (L, PAIR), 1) % HEAD_DIM
    causal = row >= jcol
    strict = row > jcol

    def prepare(wv):
        o0 = wv * R
        for s in range(n_seg):
            slot = wv * n_seg + s
            r0 = wv * rows_in + s * seg_rows
            if not blocks_per_stream:
                xp_ref[slot, 5:8, :] = cs_ref[slot]
            elif wv > 0:
                xp_ref[slot, 5:8, :] = qkv_ref[r0 - (CONV_K - 1):r0, :]
            else:
                xp_ref[slot, 5:8, :] = halo_ref[5:8, :]

                @pl.when(pl.program_id(0) % blocks_per_stream == 0)
                def _():
                    xp_ref[slot, 5:8, :] = cs_ref[...]
            xp_ref[slot, 8:8 + seg_rows, :] = qkv_ref[r0:r0 + seg_rows, :]
            for c0 in range(0, CONV_DIM, 2 * PAIR):
                cs = slice(c0, c0 + 2 * PAIR)
                y = xp_ref[slot, 5:5 + seg_rows, cs] * cw_ref[0:1, cs]
                for i in range(1, CONV_K):
                    y = y + xp_ref[slot, 5 + i:5 + i + seg_rows, cs] * cw_ref[i:i + 1, cs]
                y_ref[o0 + s * pad:o0 + s * pad + seg_rows, cs] = _silu(y)
                yield
            ab = ab_ref[r0:r0 + seg_rows, :]
            col = lax.broadcasted_iota(jnp.int32, ab.shape, 1)
            g = -jnp.exp(alog_ref[...]) * _softplus(ab + dtb_ref[...])
            gb_ref[o0 + s * pad:o0 + s * pad + seg_rows, :] = jnp.where(col < N_HEADS, g, _sigmoid(ab))
            if seg_rows < pad:
                y_ref[o0 + s * pad + seg_rows:o0 + (s + 1) * pad, :] = jnp.zeros((pad - seg_rows, CONV_DIM), F32)
                gb_ref[o0 + s * pad + seg_rows:o0 + (s + 1) * pad, :] = jnp.zeros((pad - seg_rows, GATE_PAD), F32)

        yield
        q = y_ref[o0:o0 + R, 0:WIDTH]
        q = q * lax.rsqrt(_head_sumsq(q, bdw) + NORM_EPS) * SB_SCALE
        yield
        k = y_ref[o0:o0 + R, WIDTH:2 * WIDTH]
        k = k * lax.rsqrt(_head_sumsq(k, bdw) + NORM_EPS)
        yield
        gb = gb_ref[o0:o0 + R, :]
        gcum = jnp.where(colg < N_HEADS, _mm(_dot, blk_lower, gb, 1, 3), gb)
        ex = _mm(_dot, gcum, expand_ref[...], 3, 1)
        gc_all = ex[:, 0:WIDTH]
        beta_all = ex[:, WIDTH:2 * WIDTH]
        gct_all = _mm(_dot, blk_ones, jnp.where(eye, gc_all, 0.0), 1, 2)

        for c in range(nc):
            egl_ref[wv * nc + c] = jnp.exp(gc_all[(c + 1) * L - 1:(c + 1) * L, :])
        yield
        outs, a_list, x_list = [], [], []
        for c in range(nc):
            for p in range(N_PAIRS):
                rs, sl = slice(c * L, (c + 1) * L), slice(p * PAIR, (p + 1) * PAIR)
                ro = slice(o0 + c * L, o0 + (c + 1) * L)
                qc = q[rs, sl]
                kc = k[rs, sl]
                vc = y_ref[ro, 2 * WIDTH + sl.start:2 * WIDTH + sl.stop]
                gc = gc_all[rs, sl]
                beta = beta_all[rs, sl]
                eg = jnp.exp(gc)
                decay = jnp.where(causal, jnp.exp(jnp.where(causal, gc - gct_all[rs, sl], 0.0)), 0.0)
                kb = kc * beta
                kst = _bdiag(kc.astype(BF16))
                aq = _dot_nt(jnp.concatenate([kb, qc], axis=0).astype(BF16), kst)
                outs.append((ro, sl))
                a_list.append(jnp.where(strict, aq[0:L] * decay, 0.0))
                x_list.append(jnp.concatenate([vc * beta, kb * eg], axis=1))
                qk_ref[ro, sl] = jnp.where(causal, aq[L:2 * L] * decay, 0.0).astype(BF16)
                qe_ref[ro, sl] = (qc * eg).astype(BF16)
                ks_ref[ro, sl] = (kc * jnp.exp(gc[L - 1:L, :] - gc)).astype(BF16)
                yield
        return outs, a_list, x_list

    def factor(t, a_list, x_list, between):
        pieces = 2 if t < DN_PRECISE_FACTORS else 1
        for n in range(len(a_list)):
            a_parts, x_parts = _split(a_list[n], pieces), _split(x_list[n], pieces)
            if t < 5:
                w_parts = [jnp.concatenate([_bdiag(ap), _bdiag(xp)], axis=1) for ap, xp in zip(a_parts, x_parts)]
            else:
                w_parts = [_bdiag(xp) for xp in x_parts]
            r = _dot(a_parts[0], w_parts[0])
            if pieces == 2:
                r = r + _dot(a_parts[0], w_parts[1])
            if t < 5:
                a_list[n], ax = r[:, 0:PAIR], r[:, PAIR:]
            else:
                ax = r
            x_list[n] = x_list[n] - ax if t == 0 else x_list[n] + ax
            if n % 2:
                between()

    def advance(gen, box):
        if not box:
            try:
                next(gen)
            except StopIteration as stop:
                box.append(stop.value)

    gen, box = prepare(0), []
    while not box:
        advance(gen, box)
    for wv in range(n_sub):
        outs, a_list, x_list = box[0]
        gen, box = (prepare(wv + 1), []) if wv + 1 < n_sub else (None, [None])
        for t in range(6):
            factor(t, a_list, x_list, functools.partial(advance, gen, box))
        while not box:
            advance(gen, box)
        for (ro, sl), x in zip(outs, x_list):
            u_ref[ro, sl] = x[:, 0:PAIR]
            w_ref[ro, sl] = x[:, PAIR:].astype(BF16)


def _dn_prep(qkv, ab, conv_state, wts, n_streams, seq):
    L = CHUNK
    n_sub = DN_WAVES
    if seq >= L:
        seg_rows, n_seg = DN_CHUNKS * L, 1
        blocks_per_stream = seq // (n_sub * seg_rows)
        cs_spec = pl.BlockSpec((None, CONV_K - 1, CONV_DIM), lambda i: (i // blocks_per_stream, 0, 0))
    else:
        seg_rows, n_seg, blocks_per_stream = seq, DN_CHUNKS, 0
        cs_spec = pl.BlockSpec((n_sub * n_seg, CONV_K - 1, CONV_DIM), lambda i: (i, 0, 0))
    rows_in = n_sub * n_seg * seg_rows
    R = n_sub * n_seg * max(seg_rows, L)
    n_blocks = n_streams * seq // rows_in
    tok = lambda w: pl.BlockSpec((rows_in, w), lambda i: (i, 0))
    halo = pl.BlockSpec((8, CONV_DIM), lambda i: (jnp.maximum(i * (rows_in // 8) - 1, 0), 0))
    out = lambda: pl.BlockSpec((R, WIDTH), lambda i: (i, 0))
    n_rows = n_blocks * R
    body = functools.partial(_dn_prep_body, seg_rows=seg_rows, n_seg=n_seg, n_sub=n_sub,
                             blocks_per_stream=blocks_per_stream)
    return pl.pallas_call(
        body,
        grid=(n_blocks,),
        in_specs=[tok(CONV_DIM), halo, tok(GATE_PAD), cs_spec,
                  _const_spec((CONV_K, CONV_DIM)), _const_spec((1, GATE_PAD)), _const_spec((1, GATE_PAD)),
                  _const_spec((WIDTH, WIDTH)), _const_spec((GATE_PAD, 2 * WIDTH))],
        out_specs=[out(), out(), out(), out(), out(), pl.BlockSpec((R // L, 1, WIDTH), lambda i: (i, 0, 0))],
        out_shape=[jax.ShapeDtypeStruct((n_rows, WIDTH), F32)] +
                  [jax.ShapeDtypeStruct((n_rows, WIDTH), BF16)] * 4 +
                  [jax.ShapeDtypeStruct((n_rows // L, 1, WIDTH), F32)],
        scratch_shapes=[pltpu.VMEM((n_sub * n_seg, seg_rows + 8, CONV_DIM), F32), pltpu.VMEM((R, CONV_DIM), F32),
                        pltpu.VMEM((R, GATE_PAD), F32)],
        compiler_params=_params(("parallel",)),
        name="dn_prep",
    )(qkv, qkv, ab, conv_state, wts["conv_w"], wts["alog"], wts["dtb"], wts["bd_w"], wts["expand"])


def _dn_scan_body(u_ref, w_ref, qe_ref, ks_ref, qk_ref, egl_ref, z_ref, s0_ref, gout_ref, bdw_ref,
                  o_ref, s_out_ref, s_ref, oraw_ref, *, rows, nb, n_steps):
    c = pl.program_id(1)
    L = CHUNK

    @pl.when(c == 0)
    def _():
        zero = jnp.zeros((HEAD_DIM, HEAD_DIM), F32)
        for b in range(nb):
            for p in range(N_PAIRS):
                s_ref[b, p] = jnp.concatenate([jnp.concatenate([s0_ref[b, 2 * p], zero], axis=1),
                                               jnp.concatenate([zero, s0_ref[b, 2 * p + 1]], axis=1)], axis=0)

    r = lax.broadcasted_iota(jnp.int32, (PAIR, PAIR), 0)
    cl = lax.broadcasted_iota(jnp.int32, (PAIR, PAIR), 1)
    same_head = (r // HEAD_DIM) == (cl // HEAD_DIM)
    chains = [(b, p, slice(p * PAIR, (p + 1) * PAIR)) for b in range(nb) for p in range(N_PAIRS)]
    group = 4 * N_PAIRS
    for g0 in range(0, len(chains), group):
        grp = chains[g0:g0 + group]
        ws = [_dot(jnp.concatenate([w_ref[b, :, sl], qe_ref[b, :, sl]], axis=0), s_ref[b, p].astype(BF16))
              for b, p, sl in grp]
        v_new = [(u_ref[b, :, sl] - r[0:L]).astype(BF16) for (b, p, sl), r in zip(grp, ws)]
        for (b, p, sl), r, v in zip(grp, ws, v_new):
            oraw_ref[b, :, sl] = r[L:2 * L] + _dot(qk_ref[b, :, sl], _bdiag(v))
        for (b, p, sl), v in zip(grp, v_new):
            s_ref[b, p] = (s_ref[b, p] * egl_ref[b, 0, :, sl]
                           + jnp.where(same_head, _dot_tn(ks_ref[b, :, sl], v), 0.0))

    o = oraw_ref[...].reshape(nb * L, WIDTH)
    o = o * lax.rsqrt(_head_sumsq(o, bdw_ref[...]) * (1.0 / HEAD_DIM) + NORM_EPS) * gout_ref[...]
    for b in range(nb):
        o_ref[b] = (o[b * L:b * L + rows] * _silu(z_ref[b])).astype(BF16)

    @pl.when(c == n_steps - 1)
    def _():
        for b in range(nb):
            for p in range(N_PAIRS):
                s_out_ref[b, 2 * p] = s_ref[b, p, 0:HEAD_DIM, 0:HEAD_DIM]
                s_out_ref[b, 2 * p + 1] = s_ref[b, p, HEAD_DIM:PAIR, HEAD_DIM:PAIR]


def _dn_scan(prep, z, s0, wts, n_streams, seq):
    L = CHUNK
    rows = min(L, seq)
    n_steps = seq // rows
    nb = DN_STREAMS
    u, w, qe, ks, qk, egl = prep
    as3 = lambda a: a.reshape(n_streams, n_steps * L, WIDTH)
    blk = lambda: pl.BlockSpec((nb, L, WIDTH), lambda g, c: (g, c, 0))
    tok = lambda: pl.BlockSpec((nb, rows, WIDTH), lambda g, c: (g, c, 0))
    st = lambda: pl.BlockSpec((nb, N_HEADS, HEAD_DIM, HEAD_DIM), lambda g, c: (g, 0, 0, 0))
    body = functools.partial(_dn_scan_body, rows=rows, nb=nb, n_steps=n_steps)
    o, s_new = pl.pallas_call(
        body,
        grid=(n_streams // nb, n_steps),
        in_specs=[blk(), blk(), blk(), blk(), blk(),
                  pl.BlockSpec((nb, 1, 1, WIDTH), lambda g, c: (g, c, 0, 0)), tok(), st(),
                  _const_spec((1, WIDTH)), _const_spec((WIDTH, WIDTH))],
        out_specs=[tok(), st()],
        out_shape=[jax.ShapeDtypeStruct((n_streams, seq, WIDTH), BF16),
                   jax.ShapeDtypeStruct((n_streams, N_HEADS, HEAD_DIM, HEAD_DIM), F32)],
        scratch_shapes=[pltpu.VMEM((nb, N_PAIRS, PAIR, PAIR), F32), pltpu.VMEM((nb, L, WIDTH), F32)],
        compiler_params=_params(("parallel", "arbitrary")),
        name="dn_scan",
    )(as3(u), as3(w), as3(qe), as3(ks), as3(qk), egl.reshape(n_streams, n_steps, 1, WIDTH),
      z.reshape(n_streams, seq, WIDTH), s0, wts["gout"], wts["bd_w"])
    return o.reshape(n_streams * seq, WIDTH), s_new


def _sb_blocks(probs, tri, valid, keys_minor=False):
    zs = [(_dot if keys_minor else _dot_nt)(q, k) for q, k, _, _ in probs]
    keeps = []
    for z in zs:
        nz = -z
        log_keep = jnp.minimum(nz, 0.0) - jnp.log2(1.0 + jnp.exp2(jnp.minimum(z, nz)))
        if valid is not None:
            log_keep = jnp.where(valid, log_keep, 0.0)
        keeps.append(log_keep.astype(BF16))
    sums = [_dot(lk, tri) for lk in keeps]
    out = []
    for (_, _, v, carry), z, sm in zip(probs, zs, sums):
        if isinstance(carry, int):
            carry = out[carry][1]
        a = jnp.exp2(z + sm)
        if valid is not None:
            a = jnp.where(valid, a, 0.0)
        pv = (_dot_nt if keys_minor else _dot)(a.astype(BF16), v) * jnp.exp2(carry)
        out.append((pv, carry + sm[:, 0:1]))
    return out


def _sb_live(carries):
    top = functools.reduce(jnp.maximum, carries)
    return (jnp.max(top) > SB_DEAD).astype(jnp.int32)


def _sb_tri(n):
    r = lax.broadcasted_iota(jnp.int32, (n, n), 0)
    c = lax.broadcasted_iota(jnp.int32, (n, n), 1)
    return (r >= c).astype(BF16)


def _sb_prompt_body(q_ref, k_ref, v_ref, o_ref):
    i = pl.program_id(1)
    t = SB_BLOCK
    tri = _sb_tri(t)
    r = lax.broadcasted_iota(jnp.int32, (2 * t, t), 0)
    c = lax.broadcasted_iota(jnp.int32, (2 * t, t), 1)
    strict = c < (r & (t - 1))
    head0 = _lane_head0((t, PAIR))
    zero = jnp.zeros((2 * t, 1), F32)
    for p0 in range(0, N_PAIRS, SB_PAIRS):
        sls = [slice(p * PAIR, (p + 1) * PAIR) for p in range(p0, p0 + SB_PAIRS)]
        qs = [_bdiag(q_ref[:, sl]) for sl in sls]

        def probs(j, carries):
            off = pl.multiple_of(j * t, t)
            return [(q, k_ref[pl.ds(off, t), sl], v_ref[pl.ds(off, t), sl], c) for q, sl, c in zip(qs, sls, carries)]

        first = _sb_blocks(probs(i, [zero] * SB_PAIRS), tri, strict)

        def more(st):
            return jnp.logical_and(st[0] < i, st[1] > 0)

        def step(st):
            n, _, cur = st
            res = _sb_blocks(probs(i - 1 - n, [c for _, c in cur]), tri, None)
            new = tuple((acc + pv, c) for (acc, _), (pv, c) in zip(cur, res))
            return n + 1, _sb_live([c for _, c in new]), new

        _, _, last = lax.while_loop(more, step, (jnp.int32(0), _sb_live([c for _, c in first]), tuple(first)))
        for sl, (acc, _) in zip(sls, last):
            o_ref[:, sl] = jnp.where(head0, acc[0:t], acc[t:2 * t]).astype(BF16)


def _sb_prompt(q16, k16, v16, batch, seq):
    t = SB_BLOCK
    nq = seq // t
    q3 = q16.reshape(batch, seq, WIDTH)
    k3 = k16.reshape(batch, seq, WIDTH)
    v3 = v16.reshape(batch, seq, WIDTH)
    full = pl.BlockSpec((None, seq, WIDTH), lambda b, i: (b, 0, 0))
    blk = pl.BlockSpec((None, t, WIDTH), lambda b, i: (b, i, 0))
    out = pl.pallas_call(
        _sb_prompt_body,
        grid=(batch, nq),
        in_specs=[blk, full, full],
        out_specs=blk,
        out_shape=jax.ShapeDtypeStruct((batch, seq, WIDTH), BF16),
        compiler_params=_params(("parallel", "arbitrary")),
        name="sb_prompt",
    )(q3, k3, v3)
    return out.reshape(batch * seq, WIDTH)


def _sb_sample_body(q_ref, kn_ref, vn_ref, kl_ref, vl_ref, kall_ref, vall_ref, o_ref, kbuf, vbuf, sem,
                    *, tq, tk, n_blocks):
    b = pl.program_id(0)
    t = SB_BLOCK
    m = N_HEADS * tq
    tri = _sb_tri(t)
    q = q_ref[...]
    lane = lax.broadcasted_iota(jnp.int32, (tq, WIDTH), 1)
    head_lanes = [(lane >= h * HEAD_DIM) & (lane < (h + 1) * HEAD_DIM) for h in range(N_HEADS)]
    qs = jnp.concatenate([jnp.where(hm, q, jnp.zeros((), BF16)) for hm in head_lanes], axis=0)

    kn = jnp.concatenate([kn_ref[...], jnp.zeros((t - tq, WIDTH), BF16)], axis=0)
    vn = jnp.concatenate([vn_ref[...], jnp.zeros((t - tq, WIDTH), BF16)], axis=0)
    r = lax.broadcasted_iota(jnp.int32, (m, t), 0)
    c = lax.broadcasted_iota(jnp.int32, (m, t), 1)
    (acc, carry), = _sb_blocks([(qs, kn, vn, jnp.zeros((m, 1), F32))], tri, c < (r & (tq - 1)))

    def past(k_ref, v_ref, acc, carry):
        subs = list(reversed(range(tk // t)))
        res = _sb_blocks([(qs, k_ref[:, s * t:(s + 1) * t].astype(BF16), v_ref[:, s * t:(s + 1) * t].astype(BF16),
                           carry if n == 0 else n - 1) for n, s in enumerate(subs)], tri, None, keys_minor=True)
        for pv, _ in res:
            acc = acc + pv
        return acc, res[-1][1]

    acc, carry = past(kl_ref, vl_ref, acc, carry)

    def more(st):
        return jnp.logical_and(st[0] >= 0, st[1] > 0)

    def step(st):
        blk, _, acc, carry = st
        cols = pl.ds(pl.multiple_of(blk * tk, tk), tk)
        copies = [pltpu.make_async_copy(src.at[b, :, cols], dst, sem.at[n])
                  for n, (src, dst) in enumerate(((kall_ref, kbuf), (vall_ref, vbuf)))]
        for cp in copies:
            cp.start()
        for cp in copies:
            cp.wait()
        acc, carry = past(kbuf, vbuf, acc, carry)
        return blk - 1, _sb_live([carry]), acc, carry

    _, _, acc, _ = lax.while_loop(more, step, (jnp.int32(n_blocks - 2), _sb_live([carry]), acc, carry))

    out = jnp.zeros((tq, WIDTH), F32)
    for h, hm in enumerate(head_lanes):
        out = out + jnp.where(hm, acc[h * tq:(h + 1) * tq, :], 0.0)
    o_ref[...] = out.astype(BF16)


def _sb_sample(q16, k16, v16, past_k, past_v, batch, tq, tk):
    past = past_k.shape[2]
    n_blocks = past // tk
    assert tq & (tq - 1) == 0 and tq <= SB_BLOCK and past % tk == 0
    new = lambda: pl.BlockSpec((None, tq, WIDTH), lambda b: (b, 0, 0))
    latest = lambda: pl.BlockSpec((None, WIDTH, tk), lambda b: (b, 0, n_blocks - 1))
    whole = lambda: pl.BlockSpec(memory_space=pl.ANY)
    body = functools.partial(_sb_sample_body, tq=tq, tk=tk, n_blocks=n_blocks)
    out = pl.pallas_call(
        body,
        grid=(batch,),
        in_specs=[new(), new(), new(), latest(), latest(), whole(), whole()],
        out_specs=new(),
        out_shape=jax.ShapeDtypeStruct((batch, tq, WIDTH), BF16),
        scratch_shapes=[pltpu.VMEM((WIDTH, tk), F32), pltpu.VMEM((WIDTH, tk), F32), pltpu.SemaphoreType.DMA((2,))],
        compiler_params=_params(("arbitrary",)),
        name="sb_sample",
    )(q16.reshape(batch, tq, WIDTH), k16.reshape(batch, tq, WIDTH), v16.reshape(batch, tq, WIDTH),
      past_k, past_v, past_k, past_v)
    return out.reshape(batch * tq, WIDTH)


def _post_body(x_ref, oa_ref, ob_ref, p_ref, wo_ref, gm_ref, wu_ref, wd_ref, gp_ref, wg_ref, wp_ref, y_ref):
    def rms(h, g):
        ms = jnp.mean(h * h, axis=-1, keepdims=True)
        return (h * lax.rsqrt(ms + NORM_EPS) * g).astype(BF16)

    y_ref[...] = x_ref[...] + _dot(oa_ref[...], wo_ref[0:WIDTH, :]) + _dot(ob_ref[...], wo_ref[WIDTH:2 * WIDTH, :])
    h = y_ref[...]
    u = rms(h, gm_ref[...])
    ff = D_MODEL
    for j in range(D_FF // ff):
        up = jnp.maximum(_dot(u, wu_ref[:, j * ff:(j + 1) * ff]), 0.0)
        h = h + _dot((up * up).astype(BF16), wd_ref[j * ff:(j + 1) * ff, :])
    y_ref[...] = h
    h = y_ref[...]
    gate = _sigmoid(_dot(rms(h, gp_ref[...]), wg_ref[...]))
    y_ref[...] = h + gate * _dot(p_ref[...].astype(BF16), wp_ref[...])


def _post(x2d, oa, ob, p2d, w_out, g_mlp, w_up, w_down, g_ple, w_gate, w_proj, tm):
    n = x2d.shape[0]
    row = lambda w: pl.BlockSpec((tm, w), lambda i: (i, 0))
    cs = lambda a: pl.BlockSpec(a.shape, lambda i: (0, 0), pipeline_mode=pl.Buffered(1))
    return pl.pallas_call(
        _post_body,
        grid=(n // tm,),
        in_specs=[row(D_MODEL), row(WIDTH), row(WIDTH), row(PLE_DIM), cs(w_out), cs(g_mlp), cs(w_up), cs(w_down),
                  cs(g_ple), cs(w_gate), cs(w_proj)],
        out_specs=row(D_MODEL),
        out_shape=jax.ShapeDtypeStruct((n, D_MODEL), F32),
        compiler_params=_params(("parallel",)),
        name="post",
    )(x2d, oa, ob, p2d, w_out, g_mlp, w_up, w_down, g_ple, w_gate, w_proj)


def _expand_matrix():
    lane = jnp.arange(2 * WIDTH)
    src = (lane // WIDTH) * N_HEADS + (lane % WIDTH) // HEAD_DIM
    return (jnp.arange(GATE_PAD)[:, None] == src[None, :]).astype(BF16)


def _seq_minor(a):
    b, sq, h, d = a.shape
    return a.transpose(0, 2, 3, 1).reshape(b, h * d, sq)


def _group(x, p, conv_state, s0, past_k, past_v, wts, tm, sb_tk):
    batch, seq, _ = x.shape
    n = batch * seq
    x2d = x.reshape(n, D_MODEL)
    qkv, z, ab, q16, k, v, k16, v16 = _inproj(x2d, wts["g_mix"], wts["w_main"], wts["w_ab"], wts["bd_w"],
                                              wts["gq"], wts["gk"], tm, seq)
    prep = _dn_prep(qkv, ab, conv_state, wts, batch, seq)
    o_a, s_new = _dn_scan(prep, z, s0, wts, batch, seq)
    new_conv = qkv.reshape(batch, seq, CONV_DIM)[:, seq - (CONV_K - 1):, :]
    if past_k is None:
        o_b = _sb_prompt(q16, k16, v16, batch, seq)
    else:
        o_b = _sb_sample(q16, k16, v16, past_k, past_v, batch, seq, sb_tk)
    y = _post(x2d, o_a, o_b, p.reshape(n, PLE_DIM), wts["w_out"], wts["g_mlp"], wts["w_up"], wts["w_down"],
              wts["g_ple"], wts["w_gate"], wts["w_proj"], tm)
    if k.ndim == 3:
        heads = lambda a: a.reshape(batch, N_HEADS, HEAD_DIM, seq).transpose(0, 3, 1, 2)
    else:
        heads = lambda a: a.reshape(batch, seq, N_HEADS, HEAD_DIM)
    return y.reshape(batch, seq, D_MODEL), new_conv, s_new, heads(k), heads(v)


def _layer_weights(i, g_mix, w_in, conv_w, a_log, dt_bias, g_out_dn, g_q_sb, g_k_sb, w_out, g_mlp, w_up, w_down,
                   g_ple, w_ple_gate, w_ple_proj):
    w = w_in[i]
    gate0 = CONV_DIM + WIDTH
    sb0 = gate0 + 2 * N_HEADS
    pad_row = lambda a: jnp.pad(a.astype(F32), (0, GATE_PAD - a.shape[0])).reshape(1, GATE_PAD)
    return {
        "g_mix": g_mix[i].reshape(1, D_MODEL),
        "w_main": jnp.concatenate([w[:, :gate0], w[:, sb0:]], axis=1).astype(BF16),
        "w_ab": jnp.pad(w[:, gate0:sb0], ((0, 0), (0, GATE_PAD - 2 * N_HEADS))).astype(BF16),
        "bd_w": _head_blockdiag(WIDTH, HEAD_DIM),
        "expand": _expand_matrix(),
        "gq": jnp.tile(g_q_sb[i], N_HEADS).reshape(1, WIDTH),
        "gk": jnp.tile(g_k_sb[i], N_HEADS).reshape(1, WIDTH),
        "conv_w": conv_w[i],
        "alog": pad_row(a_log[i]),
        "dtb": pad_row(dt_bias[i]),
        "gout": jnp.tile(g_out_dn[i], N_HEADS).reshape(1, WIDTH),
        "w_out": w_out[i].astype(BF16),
        "g_mlp": g_mlp[i].reshape(1, D_MODEL),
        "w_up": w_up[i].astype(BF16),
        "w_down": w_down[i].astype(BF16),
        "g_ple": g_ple[i].reshape(1, D_MODEL),
        "w_gate": w_ple_gate[i].astype(BF16),
        "w_proj": w_ple_proj[i].astype(BF16),
    }


def kernel(x_prompt, x_sample, cache_conv, state_delta, cache_k, cache_v, p_prompt, p_sample, g_mix, w_in, conv_w, a_log, dt_bias, g_out_dn, g_q_sb, g_k_sb, w_out, g_mlp, w_up, w_down, g_ple, w_ple_gate, w_ple_proj):
    depth = w_in.shape[0]
    bp = x_prompt.shape[0]
    y_p, y_s = x_prompt, x_sample
    outs_p, outs_s = [], []
    for i in range(depth):
        wts = _layer_weights(i, g_mix, w_in, conv_w, a_log, dt_bias, g_out_dn, g_q_sb, g_k_sb, w_out, g_mlp,
                             w_up, w_down, g_ple, w_ple_gate, w_ple_proj)
        y_p, *rest = _group(y_p, p_prompt[i], jnp.zeros((bp, CONV_K - 1, CONV_DIM), F32),
                            jnp.zeros((bp, N_HEADS, HEAD_DIM, HEAD_DIM), F32), None, None, wts,
                            tm=512, sb_tk=None)
        outs_p.append(rest)
        dec_b, past_len = cache_k.shape[1], cache_k.shape[2]
        y_s, *rest = _group(y_s, p_sample[i], cache_conv[i], state_delta[i],
                            _seq_minor(cache_k[i]), _seq_minor(cache_v[i]),
                            wts, tm=512, sb_tk=min(2 * SB_BLOCK, past_len))
        outs_s.append(rest)
    stack = lambda outs, j: jnp.stack([o[j] for o in outs])
    return (y_p, y_s,
            stack(outs_p, 0), stack(outs_p, 1), stack(outs_p, 2), stack(outs_p, 3),
            stack(outs_s, 0), stack(outs_s, 1), stack(outs_s, 2), stack(outs_s, 3))
```

```python
import functools
import math

import jax
import jax.numpy as jnp
from jax import lax
from jax.experimental import pallas as pl
from jax.experimental.pallas import tpu as pltpu

F32 = jnp.float32
BF16 = jnp.bfloat16

D_MODEL = 1024
HEAD_DIM = 64
N_HEADS = 8
WIDTH = N_HEADS * HEAD_DIM
N_PAIRS = N_HEADS // 2
PAIR = 2 * HEAD_DIM
CONV_K = 4
CONV_DIM = 3 * WIDTH
CHUNK = 64
D_FF = 4 * D_MODEL
PLE_DIM = 256
SB_SCALE = HEAD_DIM ** -0.5
LOG2E = math.log2(math.e)
NORM_EPS = 1e-6
GATE_PAD = 128
SB_BLOCK = 256
SB_PAIRS = 4
SB_DEAD = -160.0
VMEM_LIMIT = 56 * 1024 * 1024


def _dot(a, b):
    return lax.dot_general(a, b, (((1,), (0,)), ((), ())), preferred_element_type=F32)


def _dot_nt(a, b):
    return lax.dot_general(a, b, (((1,), (1,)), ((), ())), preferred_element_type=F32)


def _dot_tn(a, b):
    return lax.dot_general(a, b, (((0,), (0,)), ((), ())), preferred_element_type=F32)


def _split(x, n):
    parts = []
    r = x
    for i in range(n):
        p = r.astype(BF16)
        parts.append(p)
        if i + 1 < n:
            r = r - p.astype(F32)
    return parts


def _mm(dot, a, b, na, nb):
    ap = [a] if a.dtype == BF16 else _split(a, na)
    bp = [b] if b.dtype == BF16 else _split(b, nb)
    out = None
    for i, x in enumerate(ap):
        for j, y in enumerate(bp):
            if i + j >= max(len(ap), len(bp)):
                continue
            t = dot(x, y)
            out = t if out is None else out + t
    return out


def _head_sumsq(t, bd):
    return _dot((t * t).astype(BF16), bd)


def _softplus(x):
    return jnp.maximum(x, 0.0) + jnp.log1p(jnp.exp(-jnp.abs(x)))


def _sigmoid(x):
    return 1.0 / (1.0 + jnp.exp(-x))


def _silu(x):
    return x * _sigmoid(x)


def _const_spec(shape):
    n = len(shape)
    return pl.BlockSpec(shape, lambda *_: (0,) * n)


def _params(sem):
    return pltpu.CompilerParams(dimension_semantics=sem, vmem_limit_bytes=VMEM_LIMIT)


def _head_blockdiag(n, blk):
    r = jnp.arange(n) // blk
    return (r[:, None] == r[None, :]).astype(BF16)


def _inproj_body(x_ref, g_ref, w_ref, wab_ref, bd_ref, gq_ref, gk_ref,
                 qkv_ref, z_ref, ab_ref, q_ref, k_ref, v_ref, k16_ref, v16_ref, *, seq_minor):
    x = x_ref[...]
    ms = jnp.mean(x * x, axis=-1, keepdims=True)
    u = (x * lax.rsqrt(ms + NORM_EPS) * g_ref[...]).astype(BF16)
    qkv_ref[...] = _dot(u, w_ref[:, 0:CONV_DIM])
    z_ref[...] = _dot(u, w_ref[:, CONV_DIM:CONV_DIM + WIDTH])
    ab_ref[...] = _dot(u, wab_ref[...])
    o = CONV_DIM + WIDTH
    bd = bd_ref[...]

    def headnorm(t, g):
        ss = _head_sumsq(t, bd)
        return t * lax.rsqrt(ss * (1.0 / HEAD_DIM) + NORM_EPS) * g

    q = headnorm(_dot(u, w_ref[:, o:o + WIDTH]), gq_ref[...])
    k = headnorm(_dot(u, w_ref[:, o + WIDTH:o + 2 * WIDTH]), gk_ref[...])
    v = _dot(u, w_ref[:, o + 2 * WIDTH:o + 3 * WIDTH])
    q_ref[...] = (q * (SB_SCALE * LOG2E)).astype(BF16)
    tm = k.shape[0]
    if seq_minor:
        k_ref[...] = k.T
        v_ref[...] = v.T
    else:
        for h in range(N_HEADS):
            k_ref[pl.ds(h, tm, stride=N_HEADS), :] = k[:, h * HEAD_DIM:(h + 1) * HEAD_DIM]
            v_ref[pl.ds(h, tm, stride=N_HEADS), :] = v[:, h * HEAD_DIM:(h + 1) * HEAD_DIM]
    k16_ref[...] = k.astype(BF16)
    v16_ref[...] = v.astype(BF16)


def _inproj(x2d, g_mix, w_main, w_ab, bd, gq, gk, tm, seq):
    n = x2d.shape[0]
    seq_minor = seq % tm == 0
    row = lambda w: pl.BlockSpec((tm, w), lambda i: (i, 0))
    if seq_minor:
        tiles = seq // tm
        kv_spec = pl.BlockSpec((None, WIDTH, tm), lambda i: (i // tiles, 0, i % tiles))
        kv_shape = jax.ShapeDtypeStruct((n // seq, WIDTH, seq), F32)
    else:
        kv_spec = pl.BlockSpec((tm * N_HEADS, HEAD_DIM), lambda i: (i, 0))
        kv_shape = jax.ShapeDtypeStruct((n * N_HEADS, HEAD_DIM), F32)
    outs = [(CONV_DIM, F32), (WIDTH, F32), (GATE_PAD, F32), (WIDTH, BF16), None, None, (WIDTH, BF16), (WIDTH, BF16)]
    return pl.pallas_call(
        functools.partial(_inproj_body, seq_minor=seq_minor),
        grid=(n // tm,),
        in_specs=[row(D_MODEL), _const_spec((1, D_MODEL)), _const_spec(w_main.shape), _const_spec(w_ab.shape),
                  _const_spec(bd.shape), _const_spec((1, WIDTH)), _const_spec((1, WIDTH))],
        out_specs=[kv_spec if o is None else row(o[0]) for o in outs],
        out_shape=[kv_shape if o is None else jax.ShapeDtypeStruct((n, o[0]), o[1]) for o in outs],
        compiler_params=_params(("parallel",)),
        name="inproj",
    )(x2d, g_mix, w_main, w_ab, bd, gq, gk)


DN_CHUNKS = 4
DN_WAVES = 2
DN_STREAMS = 8
DN_PRECISE_FACTORS = 3


def _lane_head0(shape):
    return lax.broadcasted_iota(jnp.int32, shape, len(shape) - 1) % PAIR < HEAD_DIM


def _bdiag(x):
    m0 = _lane_head0(x.shape)
    zero = jnp.zeros((), x.dtype)
    return jnp.concatenate([jnp.where(m0, x, zero), jnp.where(m0, zero, x)], axis=0)


def _dn_prep_body(qkv_ref, halo_ref, ab_ref, cs_ref, cw_ref, alog_ref, dtb_ref, bdw_ref, expand_ref,
                  u_ref, w_ref, qe_ref, ks_ref, qk_ref, egl_ref,
                  xp_ref, y_ref, gb_ref, *, seg_rows, n_seg, n_sub, blocks_per_stream):
    L = CHUNK
    pad = max(seg_rows, L)
    R = n_seg * pad
    nc = R // L
    rows_in = n_seg * seg_rows

    bdw = bdw_ref[...]
    rr = lax.broadcasted_iota(jnp.int32, (R, R), 0)
    cc = lax.broadcasted_iota(jnp.int32, (R, R), 1)
    same = (rr // L) == (cc // L)
    blk_lower = (same & (rr >= cc)).astype(BF16)
    blk_ones = same.astype(BF16)
    colg = lax.broadcasted_iota(jnp.int32, (R, GATE_PAD), 1)
    r512 = lax.broadcasted_iota(jnp.int32, (R, WIDTH), 0)
    l512 = lax.broadcasted_iota(jnp.int32, (R, WIDTH), 1)
    eye = (r512 % L) == (l512 % HEAD_DIM)
    row = lax.broadcasted_iota(jnp.int32, (L, PAIR), 0)
    jcol = lax.broadcasted_iota(jnp.int32, (L, PAIR), 1) % HEAD_DIM
    causal = row >= jcol
    strict = row > jcol

    def prepare(wv):
        o0 = wv * R
        for s in range(n_seg):
            slot = wv * n_seg + s
            r0 = wv * rows_in + s * seg_rows
            if not blocks_per_stream:
                xp_ref[slot, 5:8, :] = cs_ref[slot]
            elif wv > 0:
                xp_ref[slot, 5:8, :] = qkv_ref[r0 - (CONV_K - 1):r0, :]
            else:
                xp_ref[slot, 5:8, :] = halo_ref[5:8, :]

                @pl.when(pl.program_id(0) % blocks_per_stream == 0)
                def _():
                    xp_ref[slot, 5:8, :] = cs_ref[...]
            xp_ref[slot, 8:8 + seg_rows, :] = qkv_ref[r0:r0 + seg_rows, :]
            for c0 in range(0, CONV_DIM, 2 * PAIR):
                cs = slice(c0, c0 + 2 * PAIR)
                y = xp_ref[slot, 5:5 + seg_rows, cs] * cw_ref[0:1, cs]
                for i in range(1, CONV_K):
                    y = y + xp_ref[slot, 5 + i:5 + i + seg_rows, cs] * cw_ref[i:i + 1, cs]
                y_ref[o0 + s * pad:o0 + s * pad + seg_rows, cs] = _silu(y)
                yield
            ab = ab_ref[r0:r0 + seg_rows, :]
            col = lax.broadcasted_iota(jnp.int32, ab.shape, 1)
            g = -jnp.exp(alog_ref[...]) * _softplus(ab + dtb_ref[...])
            gb_ref[o0 + s * pad:o0 + s * pad + seg_rows, :] = jnp.where(col < N_HEADS, g, _sigmoid(ab))
            if seg_rows < pad:
                y_ref[o0 + s * pad + seg_rows:o0 + (s + 1) * pad, :] = jnp.zeros((pad - seg_rows, CONV_DIM), F32)
                gb_ref[o0 + s * pad + seg_rows:o0 + (s + 1) * pad, :] = jnp.zeros((pad - seg_rows, GATE_PAD), F32)

        yield
        q = y_ref[o0:o0 + R, 0:WIDTH]
        q = q * lax.rsqrt(_head_sumsq(q, bdw) + NORM_EPS) * SB_SCALE
        yield
        k = y_ref[o0:o0 + R, WIDTH:2 * WIDTH]
        k = k * lax.rsqrt(_head_sumsq(k, bdw) + NORM_EPS)
        yield
        gb = gb_ref[o0:o0 + R, :]
        gcum = jnp.where(colg < N_HEADS, _mm(_dot, blk_lower, gb, 1, 3), gb)
        ex = _mm(_dot, gcum, expand_ref[...], 3, 1)
        gc_all = ex[:, 0:WIDTH]
        beta_all = ex[:, WIDTH:2 * WIDTH]
        gct_all = _mm(_dot, blk_ones, jnp.where(eye, gc_all, 0.0), 1, 2)

        for c in range(nc):
            egl_ref[wv * nc + c] = jnp.exp(gc_all[(c + 1) * L - 1:(c + 1) * L, :])
        yield
        outs, a_list, x_list = [], [], []
        for c in range(nc):
            for p in range(N_PAIRS):
                rs, sl = slice(c * L, (c + 1) * L), slice(p * PAIR, (p + 1) * PAIR)
                ro = slice(o0 + c * L, o0 + (c + 1) * L)
                qc = q[rs, sl]
                kc = k[rs, sl]
                vc = y_ref[ro, 2 * WIDTH + sl.start:2 * WIDTH + sl.stop]
                gc = gc_all[rs, sl]
                beta = beta_all[rs, sl]
                eg = jnp.exp(gc)
                decay = jnp.where(causal, jnp.exp(jnp.where(causal, gc - gct_all[rs, sl], 0.0)), 0.0)
                kb = kc * beta
                kst = _bdiag(kc.astype(BF16))
                aq = _dot_nt(jnp.concatenate([kb, qc], axis=0).astype(BF16), kst)
                outs.append((ro, sl))
                a_list.append(jnp.where(strict, aq[0:L] * decay, 0.0))
                x_list.append(jnp.concatenate([vc * beta, kb * eg], axis=1))
                qk_ref[ro, sl] = jnp.where(causal, aq[L:2 * L] * decay, 0.0).astype(BF16)
                qe_ref[ro, sl] = (qc * eg).astype(BF16)
                ks_ref[ro, sl] = (kc * jnp.exp(gc[L - 1:L, :] - gc)).astype(BF16)
                yield
        return outs, a_list, x_list

    def factor(t, a_list, x_list, between):
        pieces = 2 if t < DN_PRECISE_FACTORS else 1
        for n in range(len(a_list)):
            a_parts, x_parts = _split(a_list[n], pieces), _split(x_list[n], pieces)
            if t < 5:
                w_parts = [jnp.concatenate([_bdiag(ap), _bdiag(xp)], axis=1) for ap, xp in zip(a_parts, x_parts)]
            else:
                w_parts = [_bdiag(xp) for xp in x_parts]
            r = _dot(a_parts[0], w_parts[0])
            if pieces == 2:
                r = r + _dot(a_parts[0], w_parts[1])
            if t < 5:
                a_list[n], ax = r[:, 0:PAIR], r[:, PAIR:]
            else:
                ax = r
            x_list[n] = x_list[n] - ax if t == 0 else x_list[n] + ax
            if n % 2:
                between()

    def advance(gen, box):
        if not box:
            try:
                next(gen)
            except StopIteration as stop:
                box.append(stop.value)

    gen, box = prepare(0), []
    while not box:
        advance(gen, box)
    for wv in range(n_sub):
        outs, a_list, x_list = box[0]
        gen, box = (prepare(wv + 1), []) if wv + 1 < n_sub else (None, [None])
        for t in range(6):
            factor(t, a_list, x_list, functools.partial(advance, gen, box))
        while not box:
            advance(gen, box)
        for (ro, sl), x in zip(outs, x_list):
            u_ref[ro, sl] = x[:, 0:PAIR]
            w_ref[ro, sl] = x[:, PAIR:].astype(BF16)


def _dn_prep(qkv, ab, conv_state, wts, n_streams, seq):
    L = CHUNK
    n_sub = DN_WAVES
    if seq >= L:
        seg_rows, n_seg = DN_CHUNKS * L, 1
        blocks_per_stream = seq // (n_sub * seg_rows)
        cs_spec = pl.BlockSpec((None, CONV_K - 1, CONV_DIM), lambda i: (i // blocks_per_stream, 0, 0))
    else:
        seg_rows, n_seg, blocks_per_stream = seq, DN_CHUNKS, 0
        cs_spec = pl.BlockSpec((n_sub * n_seg, CONV_K - 1, CONV_DIM), lambda i: (i, 0, 0))
    rows_in = n_sub * n_seg * seg_rows
    R = n_sub * n_seg * max(seg_rows, L)
    n_blocks = n_streams * seq // rows_in
    tok = lambda w: pl.BlockSpec((rows_in, w), lambda i: (i, 0))
    halo = pl.BlockSpec((8, CONV_DIM), lambda i: (jnp.maximum(i * (rows_in // 8) - 1, 0), 0))
    out = lambda: pl.BlockSpec((R, WIDTH), lambda i: (i, 0))
    n_rows = n_blocks * R
    body = functools.partial(_dn_prep_body, seg_rows=seg_rows, n_seg=n_seg, n_sub=n_sub,
                             blocks_per_stream=blocks_per_stream)
    return pl.pallas_call(
        body,
        grid=(n_blocks,),
        in_specs=[tok(CONV_DIM), halo, tok(GATE_PAD), cs_spec,
                  _const_spec((CONV_K, CONV_DIM)), _const_spec((1, GATE_PAD)), _const_spec((1, GATE_PAD)),
                  _const_spec((WIDTH, WIDTH)), _const_spec((GATE_PAD, 2 * WIDTH))],
        out_specs=[out(), out(), out(), out(), out(), pl.BlockSpec((R // L, 1, WIDTH), lambda i: (i, 0, 0))],
        out_shape=[jax.ShapeDtypeStruct((n_rows, WIDTH), F32)] +
                  [jax.ShapeDtypeStruct((n_rows, WIDTH), BF16)] * 4 +
                  [jax.ShapeDtypeStruct((n_rows // L, 1, WIDTH), F32)],
        scratch_shapes=[pltpu.VMEM((n_sub * n_seg, seg_rows + 8, CONV_DIM), F32), pltpu.VMEM((R, CONV_DIM), F32),
                        pltpu.VMEM((R, GATE_PAD), F32)],
        compiler_params=_params(("parallel",)),
        name="dn_prep",
    )(qkv, qkv, ab, conv_state, wts["conv_w"], wts["alog"], wts["dtb"], wts["bd_w"], wts["expand"])


def _dn_scan_body(u_ref, w_ref, qe_ref, ks_ref, qk_ref, egl_ref, z_ref, s0_ref, gout_ref, bdw_ref,
                  o_ref, s_out_ref, s_ref, oraw_ref, *, rows, nb, n_steps):
    c = pl.program_id(1)
    L = CHUNK

    @pl.when(c == 0)
    def _():
        zero = jnp.zeros((HEAD_DIM, HEAD_DIM), F32)
        for b in range(nb):
            for p in range(N_PAIRS):
                s_ref[b, p] = jnp.concatenate([jnp.concatenate([s0_ref[b, 2 * p], zero], axis=1),
                                               jnp.concatenate([zero, s0_ref[b, 2 * p + 1]], axis=1)], axis=0)

    r = lax.broadcasted_iota(jnp.int32, (PAIR, PAIR), 0)
    cl = lax.broadcasted_iota(jnp.int32, (PAIR, PAIR), 1)
    same_head = (r // HEAD_DIM) == (cl // HEAD_DIM)
    chains = [(b, p, slice(p * PAIR, (p + 1) * PAIR)) for b in range(nb) for p in range(N_PAIRS)]
    group = 4 * N_PAIRS
    for g0 in range(0, len(chains), group):
        grp = chains[g0:g0 + group]
        ws = [_dot(jnp.concatenate([w_ref[b, :, sl], qe_ref[b, :, sl]], axis=0), s_ref[b, p].astype(BF16))
              for b, p, sl in grp]
        v_new = [(u_ref[b, :, sl] - r[0:L]).astype(BF16) for (b, p, sl), r in zip(grp, ws)]
        for (b, p, sl), r, v in zip(grp, ws, v_new):
            oraw_ref[b, :, sl] = r[L:2 * L] + _dot(qk_ref[b, :, sl], _bdiag(v))
        for (b, p, sl), v in zip(grp, v_new):
            s_ref[b, p] = (s_ref[b, p] * egl_ref[b, 0, :, sl]
                           + jnp.where(same_head, _dot_tn(ks_ref[b, :, sl], v), 0.0))

    o = oraw_ref[...].reshape(nb * L, WIDTH)
    o = o * lax.rsqrt(_head_sumsq(o, bdw_ref[...]) * (1.0 / HEAD_DIM) + NORM_EPS) * gout_ref[...]
    for b in range(nb):
        o_ref[b] = (o[b * L:b * L + rows] * _silu(z_ref[b])).astype(BF16)

    @pl.when(c == n_steps - 1)
    def _():
        for b in range(nb):
            for p in range(N_PAIRS):
                s_out_ref[b, 2 * p] = s_ref[b, p, 0:HEAD_DIM, 0:HEAD_DIM]
                s_out_ref[b, 2 * p + 1] = s_ref[b, p, HEAD_DIM:PAIR, HEAD_DIM:PAIR]


def _dn_scan(prep, z, s0, wts, n_streams, seq):
    L = CHUNK
    rows = min(L, seq)
    n_steps = seq // rows
    nb = DN_STREAMS
    u, w, qe, ks, qk, egl = prep
    as3 = lambda a: a.reshape(n_streams, n_steps * L, WIDTH)
    blk = lambda: pl.BlockSpec((nb, L, WIDTH), lambda g, c: (g, c, 0))
    tok = lambda: pl.BlockSpec((nb, rows, WIDTH), lambda g, c: (g, c, 0))
    st = lambda: pl.BlockSpec((nb, N_HEADS, HEAD_DIM, HEAD_DIM), lambda g, c: (g, 0, 0, 0))
    body = functools.partial(_dn_scan_body, rows=rows, nb=nb, n_steps=n_steps)
    o, s_new = pl.pallas_call(
        body,
        grid=(n_streams // nb, n_steps),
        in_specs=[blk(), blk(), blk(), blk(), blk(),
                  pl.BlockSpec((nb, 1, 1, WIDTH), lambda g, c: (g, c, 0, 0)), tok(), st(),
                  _const_spec((1, WIDTH)), _const_spec((WIDTH, WIDTH))],
        out_specs=[tok(), st()],
        out_shape=[jax.ShapeDtypeStruct((n_streams, seq, WIDTH), BF16),
                   jax.ShapeDtypeStruct((n_streams, N_HEADS, HEAD_DIM, HEAD_DIM), F32)],
        scratch_shapes=[pltpu.VMEM((nb, N_PAIRS, PAIR, PAIR), F32), pltpu.VMEM((nb, L, WIDTH), F32)],
        compiler_params=_params(("parallel", "arbitrary")),
        name="dn_scan",
    )(as3(u), as3(w), as3(qe), as3(ks), as3(qk), egl.reshape(n_streams, n_steps, 1, WIDTH),
      z.reshape(n_streams, seq, WIDTH), s0, wts["gout"], wts["bd_w"])
    return o.reshape(n_streams * seq, WIDTH), s_new


def _sb_blocks(probs, tri, valid, keys_minor=False):
    valids = valid if isinstance(valid, list) else [valid] * len(probs)
    minors = keys_minor if isinstance(keys_minor, list) else [keys_minor] * len(probs)
    zs = [(_dot if km else _dot_nt)(q, k) for (q, k, _, _), km in zip(probs, minors)]
    keeps = []
    for z, ok in zip(zs, valids):
        nz = -z
        log_keep = jnp.minimum(nz, 0.0) - jnp.log2(1.0 + jnp.exp2(jnp.minimum(z, nz)))
        if ok is not None:
            log_keep = jnp.where(ok, log_keep, 0.0)
        keeps.append(log_keep.astype(BF16))
    sums = [_dot(lk, tri) for lk in keeps]
    out = []
    for (_, _, v, carry), z, sm, ok, km in zip(probs, zs, sums, valids, minors):
        if isinstance(carry, int):
            carry = out[carry][1]
        a = jnp.exp2(z + sm)
        if ok is not None:
            a = jnp.where(ok, a, 0.0)
        pv = (_dot_nt if km else _dot)(a.astype(BF16), v) * jnp.exp2(carry)
        out.append((pv, carry + sm[:, 0:1]))
    return out


def _sb_live(carries):
    top = functools.reduce(jnp.maximum, carries)
    return (jnp.max(top) > SB_DEAD).astype(jnp.int32)


def _sb_tri(n):
    r = lax.broadcasted_iota(jnp.int32, (n, n), 0)
    c = lax.broadcasted_iota(jnp.int32, (n, n), 1)
    return (r >= c).astype(BF16)


def _sb_prompt_body(q_ref, k_ref, v_ref, o_ref):
    i = pl.program_id(1)
    t = SB_BLOCK
    tri = _sb_tri(t)
    r = lax.broadcasted_iota(jnp.int32, (2 * t, t), 0)
    c = lax.broadcasted_iota(jnp.int32, (2 * t, t), 1)
    strict = c < (r & (t - 1))
    head0 = _lane_head0((t, PAIR))
    zero = jnp.zeros((2 * t, 1), F32)
    for p0 in range(0, N_PAIRS, SB_PAIRS):
        sls = [slice(p * PAIR, (p + 1) * PAIR) for p in range(p0, p0 + SB_PAIRS)]
        qs = [_bdiag(q_ref[:, sl]) for sl in sls]

        def probs(j, carries):
            off = pl.multiple_of(j * t, t)
            return [(q, k_ref[pl.ds(off, t), sl], v_ref[pl.ds(off, t), sl], c) for q, sl, c in zip(qs, sls, carries)]

        first = _sb_blocks(probs(i, [zero] * SB_PAIRS), tri, strict)

        def more(st):
            return jnp.logical_and(st[0] < i, st[1] > 0)

        def step(st):
            n, _, cur = st
            res = _sb_blocks(probs(i - 1 - n, [c for _, c in cur]), tri, None)
            new = tuple((acc + pv, c) for (acc, _), (pv, c) in zip(cur, res))
            return n + 1, _sb_live([c for _, c in new]), new

        _, _, last = lax.while_loop(more, step, (jnp.int32(0), _sb_live([c for _, c in first]), tuple(first)))
        for sl, (acc, _) in zip(sls, last):
            o_ref[:, sl] = jnp.where(head0, acc[0:t], acc[t:2 * t]).astype(BF16)


def _sb_prompt(q16, k16, v16, batch, seq):
    t = SB_BLOCK
    nq = seq // t
    q3 = q16.reshape(batch, seq, WIDTH)
    k3 = k16.reshape(batch, seq, WIDTH)
    v3 = v16.reshape(batch, seq, WIDTH)
    full = pl.BlockSpec((None, seq, WIDTH), lambda b, i: (b, 0, 0))
    blk = pl.BlockSpec((None, t, WIDTH), lambda b, i: (b, i, 0))
    out = pl.pallas_call(
        _sb_prompt_body,
        grid=(batch, nq),
        in_specs=[blk, full, full],
        out_specs=blk,
        out_shape=jax.ShapeDtypeStruct((batch, seq, WIDTH), BF16),
        compiler_params=_params(("parallel", "arbitrary")),
        name="sb_prompt",
    )(q3, k3, v3)
    return out.reshape(batch * seq, WIDTH)


def _sb_sample_body(q_ref, kn_ref, vn_ref, kl_ref, vl_ref, kall_ref, vall_ref, o_ref, kbuf, vbuf, sem,
                    *, tq, tk, n_blocks):
    b = pl.program_id(0)
    t = SB_BLOCK
    m = N_HEADS * tq
    tri = _sb_tri(t)
    q = q_ref[...]
    lane = lax.broadcasted_iota(jnp.int32, (tq, WIDTH), 1)
    head_lanes = [(lane >= h * HEAD_DIM) & (lane < (h + 1) * HEAD_DIM) for h in range(N_HEADS)]
    qs = jnp.concatenate([jnp.where(hm, q, jnp.zeros((), BF16)) for hm in head_lanes], axis=0)

    kn = jnp.concatenate([kn_ref[...], jnp.zeros((t - tq, WIDTH), BF16)], axis=0)
    vn = jnp.concatenate([vn_ref[...], jnp.zeros((t - tq, WIDTH), BF16)], axis=0)
    r = lax.broadcasted_iota(jnp.int32, (m, t), 0)
    c = lax.broadcasted_iota(jnp.int32, (m, t), 1)
    def past(k_ref, v_ref, acc, carry):
        subs = list(reversed(range(tk // t)))
        res = _sb_blocks([(qs, k_ref[:, s * t:(s + 1) * t].astype(BF16), v_ref[:, s * t:(s + 1) * t].astype(BF16),
                           carry if n == 0 else n - 1) for n, s in enumerate(subs)], tri, None, keys_minor=True)
        for pv, _ in res:
            acc = acc + pv
        return acc, res[-1][1]

    assert tk == t
    (acc, _), (pv, carry) = _sb_blocks(
        [(qs, kn, vn, jnp.zeros((m, 1), F32)), (qs, kl_ref[...].astype(BF16), vl_ref[...].astype(BF16), 0)],
        tri, [c < (r & (tq - 1)), None], [False, True])
    acc = acc + pv

    def more(st):
        return jnp.logical_and(st[0] >= 0, st[1] > 0)

    def step(st):
        blk, _, acc, carry = st
        cols = pl.ds(pl.multiple_of(blk * tk, tk), tk)
        copies = [pltpu.make_async_copy(src.at[b, :, cols], dst, sem.at[n])
                  for n, (src, dst) in enumerate(((kall_ref, kbuf), (vall_ref, vbuf)))]
        for cp in copies:
            cp.start()
        for cp in copies:
            cp.wait()
        acc, carry = past(kbuf, vbuf, acc, carry)
        return blk - 1, _sb_live([carry]), acc, carry

    _, _, acc, _ = lax.while_loop(more, step, (jnp.int32(n_blocks - 2), _sb_live([carry]), acc, carry))

    out = jnp.zeros((tq, WIDTH), F32)
    for h, hm in enumerate(head_lanes):
        out = out + jnp.where(hm, acc[h * tq:(h + 1) * tq, :], 0.0)
    o_ref[...] = out.astype(BF16)


def _sb_sample(q16, k16, v16, past_k, past_v, batch, tq, tk):
    past = past_k.shape[2]
    n_blocks = past // tk
    assert tq & (tq - 1) == 0 and tq <= SB_BLOCK and past % tk == 0
    new = lambda: pl.BlockSpec((None, tq, WIDTH), lambda b: (b, 0, 0))
    latest = lambda: pl.BlockSpec((None, WIDTH, tk), lambda b: (b, 0, n_blocks - 1))
    whole = lambda: pl.BlockSpec(memory_space=pl.ANY)
    body = functools.partial(_sb_sample_body, tq=tq, tk=tk, n_blocks=n_blocks)
    out = pl.pallas_call(
        body,
        grid=(batch,),
        in_specs=[new(), new(), new(), latest(), latest(), whole(), whole()],
        out_specs=new(),
        out_shape=jax.ShapeDtypeStruct((batch, tq, WIDTH), BF16),
        scratch_shapes=[pltpu.VMEM((WIDTH, tk), F32), pltpu.VMEM((WIDTH, tk), F32), pltpu.SemaphoreType.DMA((2,))],
        compiler_params=_params(("arbitrary",)),
        name="sb_sample",
    )(q16.reshape(batch, tq, WIDTH), k16.reshape(batch, tq, WIDTH), v16.reshape(batch, tq, WIDTH),
      past_k, past_v, past_k, past_v)
    return out.reshape(batch * tq, WIDTH)


def _post_body(x_ref, oa_ref, ob_ref, p_ref, wo_ref, gm_ref, wu_ref, wd_ref, gp_ref, wg_ref, wp_ref, y_ref):
    def rms(h, g):
        ms = jnp.mean(h * h, axis=-1, keepdims=True)
        return (h * lax.rsqrt(ms + NORM_EPS) * g).astype(BF16)

    y_ref[...] = x_ref[...] + _dot(oa_ref[...], wo_ref[0:WIDTH, :]) + _dot(ob_ref[...], wo_ref[WIDTH:2 * WIDTH, :])
    h = y_ref[...]
    u = rms(h, gm_ref[...])
    ff = D_MODEL
    for j in range(D_FF // ff):
        up = jnp.maximum(_dot(u, wu_ref[:, j * ff:(j + 1) * ff]), 0.0)
        h = h + _dot((up * up).astype(BF16), wd_ref[j * ff:(j + 1) * ff, :])
    y_ref[...] = h
    h = y_ref[...]
    gate = _sigmoid(_dot(rms(h, gp_ref[...]), wg_ref[...]))
    y_ref[...] = h + gate * _dot(p_ref[...].astype(BF16), wp_ref[...])


def _post(x2d, oa, ob, p2d, w_out, g_mlp, w_up, w_down, g_ple, w_gate, w_proj, tm):
    n = x2d.shape[0]
    row = lambda w: pl.BlockSpec((tm, w), lambda i: (i, 0))
    cs = lambda a: pl.BlockSpec(a.shape, lambda i: (0, 0), pipeline_mode=pl.Buffered(1))
    return pl.pallas_call(
        _post_body,
        grid=(n // tm,),
        in_specs=[row(D_MODEL), row(WIDTH), row(WIDTH), row(PLE_DIM), cs(w_out), cs(g_mlp), cs(w_up), cs(w_down),
                  cs(g_ple), cs(w_gate), cs(w_proj)],
        out_specs=row(D_MODEL),
        out_shape=jax.ShapeDtypeStruct((n, D_MODEL), F32),
        compiler_params=_params(("parallel",)),
        name="post",
    )(x2d, oa, ob, p2d, w_out, g_mlp, w_up, w_down, g_ple, w_gate, w_proj)


def _expand_matrix():
    lane = jnp.arange(2 * WIDTH)
    src = (lane // WIDTH) * N_HEADS + (lane % WIDTH) // HEAD_DIM
    return (jnp.arange(GATE_PAD)[:, None] == src[None, :]).astype(BF16)


def _seq_minor(a):
    b, sq, h, d = a.shape
    return a.transpose(0, 2, 3, 1).reshape(b, h * d, sq)


def _group(x, p, conv_state, s0, past_k, past_v, wts, tm, sb_tk):
    batch, seq, _ = x.shape
    n = batch * seq
    x2d = x.reshape(n, D_MODEL)
    qkv, z, ab, q16, k, v, k16, v16 = _inproj(x2d, wts["g_mix"], wts["w_main"], wts["w_ab"], wts["bd_w"],
                                              wts["gq"], wts["gk"], tm, seq)
    prep = _dn_prep(qkv, ab, conv_state, wts, batch, seq)
    o_a, s_new = _dn_scan(prep, z, s0, wts, batch, seq)
    new_conv = qkv.reshape(batch, seq, CONV_DIM)[:, seq - (CONV_K - 1):, :]
    if past_k is None:
        o_b = _sb_prompt(q16, k16, v16, batch, seq)
    else:
        o_b = _sb_sample(q16, k16, v16, past_k, past_v, batch, seq, sb_tk)
    y = _post(x2d, o_a, o_b, p.reshape(n, PLE_DIM), wts["w_out"], wts["g_mlp"], wts["w_up"], wts["w_down"],
              wts["g_ple"], wts["w_gate"], wts["w_proj"], tm)
    if k.ndim == 3:
        heads = lambda a: a.reshape(batch, N_HEADS, HEAD_DIM, seq).transpose(0, 3, 1, 2)
    else:
        heads = lambda a: a.reshape(batch, seq, N_HEADS, HEAD_DIM)
    return y.reshape(batch, seq, D_MODEL), new_conv, s_new, heads(k), heads(v)


def _layer_weights(i, g_mix, w_in, conv_w, a_log, dt_bias, g_out_dn, g_q_sb, g_k_sb, w_out, g_mlp, w_up, w_down,
                   g_ple, w_ple_gate, w_ple_proj):
    w = w_in[i]
    gate0 = CONV_DIM + WIDTH
    sb0 = gate0 + 2 * N_HEADS
    pad_row = lambda a: jnp.pad(a.astype(F32), (0, GATE_PAD - a.shape[0])).reshape(1, GATE_PAD)
    return {
        "g_mix": g_mix[i].reshape(1, D_MODEL),
        "w_main": jnp.concatenate([w[:, :gate0], w[:, sb0:]], axis=1).astype(BF16),
        "w_ab": jnp.pad(w[:, gate0:sb0], ((0, 0), (0, GATE_PAD - 2 * N_HEADS))).astype(BF16),
        "bd_w": _head_blockdiag(WIDTH, HEAD_DIM),
        "expand": _expand_matrix(),
        "gq": jnp.tile(g_q_sb[i], N_HEADS).reshape(1, WIDTH),
        "gk": jnp.tile(g_k_sb[i], N_HEADS).reshape(1, WIDTH),
        "conv_w": conv_w[i],
        "alog": pad_row(a_log[i]),
        "dtb": pad_row(dt_bias[i]),
        "gout": jnp.tile(g_out_dn[i], N_HEADS).reshape(1, WIDTH),
        "w_out": w_out[i].astype(BF16),
        "g_mlp": g_mlp[i].reshape(1, D_MODEL),
        "w_up": w_up[i].astype(BF16),
        "w_down": w_down[i].astype(BF16),
        "g_ple": g_ple[i].reshape(1, D_MODEL),
        "w_gate": w_ple_gate[i].astype(BF16),
        "w_proj": w_ple_proj[i].astype(BF16),
    }


def kernel(x_prompt, x_sample, cache_conv, state_delta, cache_k, cache_v, p_prompt, p_sample, g_mix, w_in, conv_w, a_log, dt_bias, g_out_dn, g_q_sb, g_k_sb, w_out, g_mlp, w_up, w_down, g_ple, w_ple_gate, w_ple_proj):
    depth = w_in.shape[0]
    bp = x_prompt.shape[0]
    y_p, y_s = x_prompt, x_sample
    outs_p, outs_s = [], []
    for i in range(depth):
        wts = _layer_weights(i, g_mix, w_in, conv_w, a_log, dt_bias, g_out_dn, g_q_sb, g_k_sb, w_out, g_mlp,
                             w_up, w_down, g_ple, w_ple_gate, w_ple_proj)
        y_p, *rest = _group(y_p, p_prompt[i], jnp.zeros((bp, CONV_K - 1, CONV_DIM), F32),
                            jnp.zeros((bp, N_HEADS, HEAD_DIM, HEAD_DIM), F32), None, None, wts,
                            tm=512, sb_tk=None)
        outs_p.append(rest)
        dec_b, past_len = cache_k.shape[1], cache_k.shape[2]
        y_s, *rest = _group(y_s, p_sample[i], cache_conv[i], state_delta[i],
                            _seq_minor(cache_k[i]), _seq_minor(cache_v[i]),
                            wts, tm=512, sb_tk=min(SB_BLOCK, past_len))
        outs_s.append(rest)
    stack = lambda outs, j: jnp.stack([o[j] for o in outs])
    return (y_p, y_s,
            stack(outs_p, 0), stack(outs_p, 1), stack(outs_p, 2), stack(outs_p, 3),
            stack(outs_s, 0), stack(outs_s, 1), stack(outs_s, 2), stack(outs_s, 3))
```

```python
import functools
import math

import jax
import jax.numpy as jnp
from jax import lax
from jax.experimental import pallas as pl
from jax.experimental.pallas import tpu as pltpu

F32 = jnp.float32
BF16 = jnp.bfloat16

D_MODEL = 1024
HEAD_DIM = 64
N_HEADS = 8
WIDTH = N_HEADS * HEAD_DIM
N_PAIRS = N_HEADS // 2
PAIR = 2 * HEAD_DIM
CONV_K = 4
CONV_DIM = 3 * WIDTH
CHUNK = 64
D_FF = 4 * D_MODEL
PLE_DIM = 256
SB_SCALE = HEAD_DIM ** -0.5
LOG2E = math.log2(math.e)
NORM_EPS = 1e-6
GATE_PAD = 128
SB_BLOCK = 256
SB_PAIRS = 4
SB_DEAD = -160.0
POST_ROWS = 1024
VMEM_LIMIT = 56 * 1024 * 1024


def _dot(a, b):
    return lax.dot_general(a, b, (((1,), (0,)), ((), ())), preferred_element_type=F32)


def _dot_nt(a, b):
    return lax.dot_general(a, b, (((1,), (1,)), ((), ())), preferred_element_type=F32)


def _dot_tn(a, b):
    return lax.dot_general(a, b, (((0,), (0,)), ((), ())), preferred_element_type=F32)


def _split(x, n):
    parts = []
    r = x
    for i in range(n):
        p = r.astype(BF16)
        parts.append(p)
        if i + 1 < n:
            r = r - p.astype(F32)
    return parts


def _mm(dot, a, b, na, nb):
    ap = [a] if a.dtype == BF16 else _split(a, na)
    bp = [b] if b.dtype == BF16 else _split(b, nb)
    out = None
    for i, x in enumerate(ap):
        for j, y in enumerate(bp):
            if i + j >= max(len(ap), len(bp)):
                continue
            t = dot(x, y)
            out = t if out is None else out + t
    return out


def _head_sumsq(t, bd):
    return _dot((t * t).astype(BF16), bd)


def _softplus(x):
    return jnp.maximum(x, 0.0) + jnp.log1p(jnp.exp(-jnp.abs(x)))


def _sigmoid(x):
    return 1.0 / (1.0 + jnp.exp(-x))


def _silu(x):
    return x * _sigmoid(x)


def _const_spec(shape):
    n = len(shape)
    return pl.BlockSpec(shape, lambda *_: (0,) * n)


def _params(sem):
    return pltpu.CompilerParams(dimension_semantics=sem, vmem_limit_bytes=VMEM_LIMIT)


def _head_blockdiag(n, blk):
    r = jnp.arange(n) // blk
    return (r[:, None] == r[None, :]).astype(BF16)


def _inproj_body(x_ref, g_ref, w_ref, wab_ref, bd_ref, gq_ref, gk_ref,
                 qkv_ref, z_ref, ab_ref, q_ref, k_ref, v_ref, k16_ref, v16_ref, *, seq_minor):
    x = x_ref[...]
    ms = jnp.mean(x * x, axis=-1, keepdims=True)
    u = (x * lax.rsqrt(ms + NORM_EPS) * g_ref[...]).astype(BF16)
    qkv_ref[...] = _dot(u, w_ref[:, 0:CONV_DIM])
    z_ref[...] = _dot(u, w_ref[:, CONV_DIM:CONV_DIM + WIDTH])
    ab_ref[...] = _dot(u, wab_ref[...])
    o = CONV_DIM + WIDTH
    bd = bd_ref[...]

    def headnorm(t, g):
        ss = _head_sumsq(t, bd)
        return t * lax.rsqrt(ss * (1.0 / HEAD_DIM) + NORM_EPS) * g

    q = headnorm(_dot(u, w_ref[:, o:o + WIDTH]), gq_ref[...])
    k = headnorm(_dot(u, w_ref[:, o + WIDTH:o + 2 * WIDTH]), gk_ref[...])
    v = _dot(u, w_ref[:, o + 2 * WIDTH:o + 3 * WIDTH])
    q_ref[...] = (q * (SB_SCALE * LOG2E)).astype(BF16)
    tm = k.shape[0]
    if seq_minor:
        k_ref[...] = k.T
        v_ref[...] = v.T
    else:
        for h in range(N_HEADS):
            k_ref[pl.ds(h, tm, stride=N_HEADS), :] = k[:, h * HEAD_DIM:(h + 1) * HEAD_DIM]
            v_ref[pl.ds(h, tm, stride=N_HEADS), :] = v[:, h * HEAD_DIM:(h + 1) * HEAD_DIM]
    k16_ref[...] = k.astype(BF16)
    v16_ref[...] = v.astype(BF16)


def _inproj(x2d, g_mix, w_main, w_ab, bd, gq, gk, tm, seq):
    n = x2d.shape[0]
    seq_minor = seq % tm == 0
    row = lambda w: pl.BlockSpec((tm, w), lambda i: (i, 0))
    if seq_minor:
        tiles = seq // tm
        kv_spec = pl.BlockSpec((None, WIDTH, tm), lambda i: (i // tiles, 0, i % tiles))
        kv_shape = jax.ShapeDtypeStruct((n // seq, WIDTH, seq), F32)
    else:
        kv_spec = pl.BlockSpec((tm * N_HEADS, HEAD_DIM), lambda i: (i, 0))
        kv_shape = jax.ShapeDtypeStruct((n * N_HEADS, HEAD_DIM), F32)
    outs = [(CONV_DIM, F32), (WIDTH, F32), (GATE_PAD, F32), (WIDTH, BF16), None, None, (WIDTH, BF16), (WIDTH, BF16)]
    return pl.pallas_call(
        functools.partial(_inproj_body, seq_minor=seq_minor),
        grid=(n // tm,),
        in_specs=[row(D_MODEL), _const_spec((1, D_MODEL)), _const_spec(w_main.shape), _const_spec(w_ab.shape),
                  _const_spec(bd.shape), _const_spec((1, WIDTH)), _const_spec((1, WIDTH))],
        out_specs=[kv_spec if o is None else row(o[0]) for o in outs],
        out_shape=[kv_shape if o is None else jax.ShapeDtypeStruct((n, o[0]), o[1]) for o in outs],
        compiler_params=_params(("parallel",)),
        name="inproj",
    )(x2d, g_mix, w_main, w_ab, bd, gq, gk)


DN_CHUNKS = 4
DN_WAVES = 2
DN_STREAMS = 8
DN_PRECISE_FACTORS = 3


def _lane_head0(shape):
    return lax.broadcasted_iota(jnp.int32, shape, len(shape) - 1) % PAIR < HEAD_DIM


def _bdiag(x):
    m0 = _lane_head0(x.shape)
    zero = jnp.zeros((), x.dtype)
    return jnp.concatenate([jnp.where(m0, x, zero), jnp.where(m0, zero, x)], axis=0)


def _dn_prep_body(qkv_ref, halo_ref, ab_ref, cs_ref, cw_ref, alog_ref, dtb_ref, bdw_ref, expand_ref,
                  u_ref, w_ref, qe_ref, ks_ref, qk_ref, egl_ref,
                  xp_ref, y_ref, gb_ref, *, seg_rows, n_seg, n_sub, blocks_per_stream):
    L = CHUNK
    pad = max(seg_rows, L)
    R = n_seg * pad
    nc = R // L
    rows_in = n_seg * seg_rows

    bdw = bdw_ref[...]
    rr = lax.broadcasted_iota(jnp.int32, (R, R), 0)
    cc = lax.broadcasted_iota(jnp.int32, (R, R), 1)
    same = (rr // L) == (cc // L)
    blk_lower = (same & (rr >= cc)).astype(BF16)
    blk_ones = same.astype(BF16)
    colg = lax.broadcasted_iota(jnp.int32, (R, GATE_PAD), 1)
    r512 = lax.broadcasted_iota(jnp.int32, (R, WIDTH), 0)
    l512 = lax.broadcasted_iota(jnp.int32, (R, WIDTH), 1)
    eye = (r512 % L) == (l512 % HEAD_DIM)
    row = lax.broadcasted_iota(jnp.int32, (L, PAIR), 0)
    jcol = lax.broadcasted_iota(jnp.int32, (L, PAIR), 1) % HEAD_DIM
    causal = row >= jcol
    strict = row > jcol

    def prepare(wv):
        o0 = wv * R
        for s in range(n_seg):
            slot = wv * n_seg + s
            r0 = wv * rows_in + s * seg_rows
            if not blocks_per_stream:
                xp_ref[slot, 5:8, :] = cs_ref[slot]
            elif wv > 0:
                xp_ref[slot, 5:8, :] = qkv_ref[r0 - (CONV_K - 1):r0, :]
            else:
                xp_ref[slot, 5:8, :] = halo_ref[5:8, :]

                @pl.when(pl.program_id(0) % blocks_per_stream == 0)
                def _():
                    xp_ref[slot, 5:8, :] = cs_ref[...]
            xp_ref[slot, 8:8 + seg_rows, :] = qkv_ref[r0:r0 + seg_rows, :]
            for c0 in range(0, CONV_DIM, 2 * PAIR):
                cs = slice(c0, c0 + 2 * PAIR)
                y = xp_ref[slot, 5:5 + seg_rows, cs] * cw_ref[0:1, cs]
                for i in range(1, CONV_K):
                    y = y + xp_ref[slot, 5 + i:5 + i + seg_rows, cs] * cw_ref[i:i + 1, cs]
                y_ref[o0 + s * pad:o0 + s * pad + seg_rows, cs] = _silu(y)
                yield
            ab = ab_ref[r0:r0 + seg_rows, :]
            col = lax.broadcasted_iota(jnp.int32, ab.shape, 1)
            g = -jnp.exp(alog_ref[...]) * _softplus(ab + dtb_ref[...])
            gb_ref[o0 + s * pad:o0 + s * pad + seg_rows, :] = jnp.where(col < N_HEADS, g, _sigmoid(ab))
            if seg_rows < pad:
                y_ref[o0 + s * pad + seg_rows:o0 + (s + 1) * pad, :] = jnp.zeros((pad - seg_rows, CONV_DIM), F32)
                gb_ref[o0 + s * pad + seg_rows:o0 + (s + 1) * pad, :] = jnp.zeros((pad - seg_rows, GATE_PAD), F32)

        yield
        q = y_ref[o0:o0 + R, 0:WIDTH]
        q = q * lax.rsqrt(_head_sumsq(q, bdw) + NORM_EPS) * SB_SCALE
        yield
        k = y_ref[o0:o0 + R, WIDTH:2 * WIDTH]
        k = k * lax.rsqrt(_head_sumsq(k, bdw) + NORM_EPS)
        yield
        gb = gb_ref[o0:o0 + R, :]
        gcum = jnp.where(colg < N_HEADS, _mm(_dot, blk_lower, gb, 1, 3), gb)
        ex = _mm(_dot, gcum, expand_ref[...], 3, 1)
        gc_all = ex[:, 0:WIDTH]
        beta_all = ex[:, WIDTH:2 * WIDTH]
        gct_all = _mm(_dot, blk_ones, jnp.where(eye, gc_all, 0.0), 1, 2)

        for c in range(nc):
            egl_ref[wv * nc + c] = jnp.exp(gc_all[(c + 1) * L - 1:(c + 1) * L, :])
        yield
        outs, a_list, x_list = [], [], []
        for c in range(nc):
            for p in range(N_PAIRS):
                rs, sl = slice(c * L, (c + 1) * L), slice(p * PAIR, (p + 1) * PAIR)
                ro = slice(o0 + c * L, o0 + (c + 1) * L)
                qc = q[rs, sl]
                kc = k[rs, sl]
                vc = y_ref[ro, 2 * WIDTH + sl.start:2 * WIDTH + sl.stop]
                gc = gc_all[rs, sl]
                beta = beta_all[rs, sl]
                eg = jnp.exp(gc)
                decay = jnp.where(causal, jnp.exp(jnp.where(causal, gc - gct_all[rs, sl], 0.0)), 0.0)
                kb = kc * beta
                kst = _bdiag(kc.astype(BF16))
                aq = _dot_nt(jnp.concatenate([kb, qc], axis=0).astype(BF16), kst)
                outs.append((ro, sl))
                a_list.append(jnp.where(strict, aq[0:L] * decay, 0.0))
                x_list.append(jnp.concatenate([vc * beta, kb * eg], axis=1))
                qk_ref[ro, sl] = jnp.where(causal, aq[L:2 * L] * decay, 0.0).astype(BF16)
                qe_ref[ro, sl] = (qc * eg).astype(BF16)
                ks_ref[ro, sl] = (kc * jnp.exp(gc[L - 1:L, :] - gc)).astype(BF16)
                yield
        return outs, a_list, x_list

    def factor(t, a_list, x_list, between):
        pieces = 2 if t < DN_PRECISE_FACTORS else 1
        for n in range(len(a_list)):
            a_parts, x_parts = _split(a_list[n], pieces), _split(x_list[n], pieces)
            if t < 5:
                w_parts = [jnp.concatenate([_bdiag(ap), _bdiag(xp)], axis=1) for ap, xp in zip(a_parts, x_parts)]
            else:
                w_parts = [_bdiag(xp) for xp in x_parts]
            r = _dot(a_parts[0], w_parts[0])
            if pieces == 2:
                r = r + _dot(a_parts[0], w_parts[1])
            if t < 5:
                a_list[n], ax = r[:, 0:PAIR], r[:, PAIR:]
            else:
                ax = r
            x_list[n] = x_list[n] - ax if t == 0 else x_list[n] + ax
            if n % 2:
                between()

    def advance(gen, box):
        if not box:
            try:
                next(gen)
            except StopIteration as stop:
                box.append(stop.value)

    gen, box = prepare(0), []
    while not box:
        advance(gen, box)
    for wv in range(n_sub):
        outs, a_list, x_list = box[0]
        gen, box = (prepare(wv + 1), []) if wv + 1 < n_sub else (None, [None])
        for t in range(6):
            factor(t, a_list, x_list, functools.partial(advance, gen, box))
        while not box:
            advance(gen, box)
        for (ro, sl), x in zip(outs, x_list):
            u_ref[ro, sl] = x[:, 0:PAIR]
            w_ref[ro, sl] = x[:, PAIR:].astype(BF16)


def _dn_prep(qkv, ab, conv_state, wts, n_streams, seq):
    L = CHUNK
    n_sub = DN_WAVES
    if seq >= L:
        seg_rows, n_seg = DN_CHUNKS * L, 1
        blocks_per_stream = seq // (n_sub * seg_rows)
        cs_spec = pl.BlockSpec((None, CONV_K - 1, CONV_DIM), lambda i: (i // blocks_per_stream, 0, 0))
    else:
        seg_rows, n_seg, blocks_per_stream = seq, DN_CHUNKS, 0
        cs_spec = pl.BlockSpec((n_sub * n_seg, CONV_K - 1, CONV_DIM), lambda i: (i, 0, 0))
    rows_in = n_sub * n_seg * seg_rows
    R = n_sub * n_seg * max(seg_rows, L)
    n_blocks = n_streams * seq // rows_in
    tok = lambda w: pl.BlockSpec((rows_in, w), lambda i: (i, 0))
    halo = pl.BlockSpec((8, CONV_DIM), lambda i: (jnp.maximum(i * (rows_in // 8) - 1, 0), 0))
    out = lambda: pl.BlockSpec((R, WIDTH), lambda i: (i, 0))
    n_rows = n_blocks * R
    body = functools.partial(_dn_prep_body, seg_rows=seg_rows, n_seg=n_seg, n_sub=n_sub,
                             blocks_per_stream=blocks_per_stream)
    return pl.pallas_call(
        body,
        grid=(n_blocks,),
        in_specs=[tok(CONV_DIM), halo, tok(GATE_PAD), cs_spec,
                  _const_spec((CONV_K, CONV_DIM)), _const_spec((1, GATE_PAD)), _const_spec((1, GATE_PAD)),
                  _const_spec((WIDTH, WIDTH)), _const_spec((GATE_PAD, 2 * WIDTH))],
        out_specs=[out(), out(), out(), out(), out(), pl.BlockSpec((R // L, 1, WIDTH), lambda i: (i, 0, 0))],
        out_shape=[jax.ShapeDtypeStruct((n_rows, WIDTH), F32)] +
                  [jax.ShapeDtypeStruct((n_rows, WIDTH), BF16)] * 4 +
                  [jax.ShapeDtypeStruct((n_rows // L, 1, WIDTH), F32)],
        scratch_shapes=[pltpu.VMEM((n_sub * n_seg, seg_rows + 8, CONV_DIM), F32), pltpu.VMEM((R, CONV_DIM), F32),
                        pltpu.VMEM((R, GATE_PAD), F32)],
        compiler_params=_params(("parallel",)),
        name="dn_prep",
    )(qkv, qkv, ab, conv_state, wts["conv_w"], wts["alog"], wts["dtb"], wts["bd_w"], wts["expand"])


def _dn_scan_body(u_ref, w_ref, qe_ref, ks_ref, qk_ref, egl_ref, z_ref, s0_ref, gout_ref, bdw_ref,
                  o_ref, s_out_ref, s_ref, oraw_ref, *, rows, nb, n_steps):
    c = pl.program_id(1)
    L = CHUNK

    @pl.when(c == 0)
    def _():
        zero = jnp.zeros((HEAD_DIM, HEAD_DIM), F32)
        for b in range(nb):
            for p in range(N_PAIRS):
                s_ref[b, p] = jnp.concatenate([jnp.concatenate([s0_ref[b, 2 * p], zero], axis=1),
                                               jnp.concatenate([zero, s0_ref[b, 2 * p + 1]], axis=1)], axis=0)

    r = lax.broadcasted_iota(jnp.int32, (PAIR, PAIR), 0)
    cl = lax.broadcasted_iota(jnp.int32, (PAIR, PAIR), 1)
    same_head = (r // HEAD_DIM) == (cl // HEAD_DIM)
    chains = [(b, p, slice(p * PAIR, (p + 1) * PAIR)) for b in range(nb) for p in range(N_PAIRS)]
    group = 4 * N_PAIRS
    for g0 in range(0, len(chains), group):
        grp = chains[g0:g0 + group]
        ws = [_dot(jnp.concatenate([w_ref[b, :, sl], qe_ref[b, :, sl]], axis=0), s_ref[b, p].astype(BF16))
              for b, p, sl in grp]
        v_new = [(u_ref[b, :, sl] - r[0:L]).astype(BF16) for (b, p, sl), r in zip(grp, ws)]
        for (b, p, sl), r, v in zip(grp, ws, v_new):
            oraw_ref[b, :, sl] = r[L:2 * L] + _dot(qk_ref[b, :, sl], _bdiag(v))
        for (b, p, sl), v in zip(grp, v_new):
            s_ref[b, p] = (s_ref[b, p] * egl_ref[b, 0, :, sl]
                           + jnp.where(same_head, _dot_tn(ks_ref[b, :, sl], v), 0.0))

    o = oraw_ref[...].reshape(nb * L, WIDTH)
    o = o * lax.rsqrt(_head_sumsq(o, bdw_ref[...]) * (1.0 / HEAD_DIM) + NORM_EPS) * gout_ref[...]
    for b in range(nb):
        o_ref[b] = (o[b * L:b * L + rows] * _silu(z_ref[b])).astype(BF16)

    @pl.when(c == n_steps - 1)
    def _():
        for b in range(nb):
            for p in range(N_PAIRS):
                s_out_ref[b, 2 * p] = s_ref[b, p, 0:HEAD_DIM, 0:HEAD_DIM]
                s_out_ref[b, 2 * p + 1] = s_ref[b, p, HEAD_DIM:PAIR, HEAD_DIM:PAIR]


def _dn_scan(prep, z, s0, wts, n_streams, seq):
    L = CHUNK
    rows = min(L, seq)
    n_steps = seq // rows
    nb = DN_STREAMS
    u, w, qe, ks, qk, egl = prep
    as3 = lambda a: a.reshape(n_streams, n_steps * L, WIDTH)
    blk = lambda: pl.BlockSpec((nb, L, WIDTH), lambda g, c: (g, c, 0))
    tok = lambda: pl.BlockSpec((nb, rows, WIDTH), lambda g, c: (g, c, 0))
    st = lambda: pl.BlockSpec((nb, N_HEADS, HEAD_DIM, HEAD_DIM), lambda g, c: (g, 0, 0, 0))
    body = functools.partial(_dn_scan_body, rows=rows, nb=nb, n_steps=n_steps)
    o, s_new = pl.pallas_call(
        body,
        grid=(n_streams // nb, n_steps),
        in_specs=[blk(), blk(), blk(), blk(), blk(),
                  pl.BlockSpec((nb, 1, 1, WIDTH), lambda g, c: (g, c, 0, 0)), tok(), st(),
                  _const_spec((1, WIDTH)), _const_spec((WIDTH, WIDTH))],
        out_specs=[tok(), st()],
        out_shape=[jax.ShapeDtypeStruct((n_streams, seq, WIDTH), BF16),
                   jax.ShapeDtypeStruct((n_streams, N_HEADS, HEAD_DIM, HEAD_DIM), F32)],
        scratch_shapes=[pltpu.VMEM((nb, N_PAIRS, PAIR, PAIR), F32), pltpu.VMEM((nb, L, WIDTH), F32)],
        compiler_params=_params(("parallel", "arbitrary")),
        name="dn_scan",
    )(as3(u), as3(w), as3(qe), as3(ks), as3(qk), egl.reshape(n_streams, n_steps, 1, WIDTH),
      z.reshape(n_streams, seq, WIDTH), s0, wts["gout"], wts["bd_w"])
    return o.reshape(n_streams * seq, WIDTH), s_new


def _sb_blocks(probs, tri, valid, keys_minor=False):
    zs = [(_dot if keys_minor else _dot_nt)(q, k) for q, k, _, _ in probs]
    keeps = []
    for z in zs:
        nz = -z
        log_keep = jnp.minimum(nz, 0.0) - jnp.log2(1.0 + jnp.exp2(jnp.minimum(z, nz)))
        if valid is not None:
            log_keep = jnp.where(valid, log_keep, 0.0)
        keeps.append(log_keep.astype(BF16))
    sums = [_dot(lk, tri) for lk in keeps]
    out = []
    for (_, _, v, carry), z, sm in zip(probs, zs, sums):
        if isinstance(carry, int):
            carry = out[carry][1]
        a = jnp.exp2(z + sm)
        if valid is not None:
            a = jnp.where(valid, a, 0.0)
        pv = (_dot_nt if keys_minor else _dot)(a.astype(BF16), v) * jnp.exp2(carry)
        out.append((pv, carry + sm[:, 0:1]))
    return out


def _sb_live(carries):
    top = functools.reduce(jnp.maximum, carries)
    return (jnp.max(top) > SB_DEAD).astype(jnp.int32)


def _sb_tri(n):
    r = lax.broadcasted_iota(jnp.int32, (n, n), 0)
    c = lax.broadcasted_iota(jnp.int32, (n, n), 1)
    return (r >= c).astype(BF16)


def _sb_prompt_body(q_ref, k_ref, v_ref, o_ref):
    i = pl.program_id(1)
    t = SB_BLOCK
    tri = _sb_tri(t)
    r = lax.broadcasted_iota(jnp.int32, (2 * t, t), 0)
    c = lax.broadcasted_iota(jnp.int32, (2 * t, t), 1)
    strict = c < (r & (t - 1))
    head0 = _lane_head0((t, PAIR))
    zero = jnp.zeros((2 * t, 1), F32)
    for p0 in range(0, N_PAIRS, SB_PAIRS):
        sls = [slice(p * PAIR, (p + 1) * PAIR) for p in range(p0, p0 + SB_PAIRS)]
        qs = [_bdiag(q_ref[:, sl]) for sl in sls]

        def probs(j, carries):
            off = pl.multiple_of(j * t, t)
            return [(q, k_ref[pl.ds(off, t), sl], v_ref[pl.ds(off, t), sl], c) for q, sl, c in zip(qs, sls, carries)]

        first = _sb_blocks(probs(i, [zero] * SB_PAIRS), tri, strict)

        def more(st):
            return jnp.logical_and(st[0] < i, st[1] > 0)

        def step(st):
            n, _, cur = st
            res = _sb_blocks(probs(i - 1 - n, [c for _, c in cur]), tri, None)
            new = tuple((acc + pv, c) for (acc, _), (pv, c) in zip(cur, res))
            return n + 1, _sb_live([c for _, c in new]), new

        _, _, last = lax.while_loop(more, step, (jnp.int32(0), _sb_live([c for _, c in first]), tuple(first)))
        for sl, (acc, _) in zip(sls, last):
            o_ref[:, sl] = jnp.where(head0, acc[0:t], acc[t:2 * t]).astype(BF16)


def _sb_prompt(q16, k16, v16, batch, seq):
    t = SB_BLOCK
    nq = seq // t
    q3 = q16.reshape(batch, seq, WIDTH)
    k3 = k16.reshape(batch, seq, WIDTH)
    v3 = v16.reshape(batch, seq, WIDTH)
    full = pl.BlockSpec((None, seq, WIDTH), lambda b, i: (b, 0, 0))
    blk = pl.BlockSpec((None, t, WIDTH), lambda b, i: (b, i, 0))
    out = pl.pallas_call(
        _sb_prompt_body,
        grid=(batch, nq),
        in_specs=[blk, full, full],
        out_specs=blk,
        out_shape=jax.ShapeDtypeStruct((batch, seq, WIDTH), BF16),
        compiler_params=_params(("parallel", "arbitrary")),
        name="sb_prompt",
    )(q3, k3, v3)
    return out.reshape(batch * seq, WIDTH)


def _sb_sample_body(q_ref, kn_ref, vn_ref, kl_ref, vl_ref, kall_ref, vall_ref, o_ref, kbuf, vbuf, sem,
                    *, tq, tk, n_blocks):
    b = pl.program_id(0)
    t = SB_BLOCK
    m = N_HEADS * tq
    tri = _sb_tri(t)
    q = q_ref[...]
    lane = lax.broadcasted_iota(jnp.int32, (tq, WIDTH), 1)
    head_lanes = [(lane >= h * HEAD_DIM) & (lane < (h + 1) * HEAD_DIM) for h in range(N_HEADS)]
    qs = jnp.concatenate([jnp.where(hm, q, jnp.zeros((), BF16)) for hm in head_lanes], axis=0)

    kn = jnp.concatenate([kn_ref[...], jnp.zeros((t - tq, WIDTH), BF16)], axis=0)
    vn = jnp.concatenate([vn_ref[...], jnp.zeros((t - tq, WIDTH), BF16)], axis=0)
    r = lax.broadcasted_iota(jnp.int32, (m, t), 0)
    c = lax.broadcasted_iota(jnp.int32, (m, t), 1)
    (acc, carry), = _sb_blocks([(qs, kn, vn, jnp.zeros((m, 1), F32))], tri, c < (r & (tq - 1)))

    def past(k_ref, v_ref, acc, carry):
        subs = list(reversed(range(tk // t)))
        res = _sb_blocks([(qs, k_ref[:, s * t:(s + 1) * t].astype(BF16), v_ref[:, s * t:(s + 1) * t].astype(BF16),
                           carry if n == 0 else n - 1) for n, s in enumerate(subs)], tri, None, keys_minor=True)
        for pv, _ in res:
            acc = acc + pv
        return acc, res[-1][1]

    acc, carry = past(kl_ref, vl_ref, acc, carry)

    def more(st):
        return jnp.logical_and(st[0] >= 0, st[1] > 0)

    def step(st):
        blk, _, acc, carry = st
        cols = pl.ds(pl.multiple_of(blk * tk, tk), tk)
        copies = [pltpu.make_async_copy(src.at[b, :, cols], dst, sem.at[n])
                  for n, (src, dst) in enumerate(((kall_ref, kbuf), (vall_ref, vbuf)))]
        for cp in copies:
            cp.start()
        for cp in copies:
            cp.wait()
        acc, carry = past(kbuf, vbuf, acc, carry)
        return blk - 1, _sb_live([carry]), acc, carry

    _, _, acc, _ = lax.while_loop(more, step, (jnp.int32(n_blocks - 2), _sb_live([carry]), acc, carry))

    out = jnp.zeros((tq, WIDTH), F32)
    for h, hm in enumerate(head_lanes):
        out = out + jnp.where(hm, acc[h * tq:(h + 1) * tq, :], 0.0)
    o_ref[...] = out.astype(BF16)


def _sb_sample(q16, k16, v16, past_k, past_v, batch, tq, tk):
    past = past_k.shape[2]
    n_blocks = past // tk
    assert tq & (tq - 1) == 0 and tq <= SB_BLOCK and past % tk == 0
    new = lambda: pl.BlockSpec((None, tq, WIDTH), lambda b: (b, 0, 0))
    latest = lambda: pl.BlockSpec((None, WIDTH, tk), lambda b: (b, 0, n_blocks - 1))
    whole = lambda: pl.BlockSpec(memory_space=pl.ANY)
    body = functools.partial(_sb_sample_body, tq=tq, tk=tk, n_blocks=n_blocks)
    out = pl.pallas_call(
        body,
        grid=(batch,),
        in_specs=[new(), new(), new(), latest(), latest(), whole(), whole()],
        out_specs=new(),
        out_shape=jax.ShapeDtypeStruct((batch, tq, WIDTH), BF16),
        scratch_shapes=[pltpu.VMEM((WIDTH, tk), F32), pltpu.VMEM((WIDTH, tk), F32), pltpu.SemaphoreType.DMA((2,))],
        compiler_params=_params(("arbitrary",)),
        name="sb_sample",
    )(q16.reshape(batch, tq, WIDTH), k16.reshape(batch, tq, WIDTH), v16.reshape(batch, tq, WIDTH),
      past_k, past_v, past_k, past_v)
    return out.reshape(batch * tq, WIDTH)


def _post_body(x_ref, oa_ref, ob_ref, p_ref, wo_ref, gm_ref, wu_ref, wd_ref, gp_ref, wg_ref, wp_ref, y_ref):
    def rms(h, g):
        ms = jnp.mean(h * h, axis=-1, keepdims=True)
        return (h * lax.rsqrt(ms + NORM_EPS) * g).astype(BF16)

    y_ref[...] = x_ref[...] + _dot(oa_ref[...], wo_ref[0:WIDTH, :]) + _dot(ob_ref[...], wo_ref[WIDTH:2 * WIDTH, :])
    h = y_ref[...]
    u = rms(h, gm_ref[...])
    ff = D_MODEL
    for j in range(D_FF // ff):
        up = jnp.maximum(_dot(u, wu_ref[:, j * ff:(j + 1) * ff]), 0.0)
        h = h + _dot((up * up).astype(BF16), wd_ref[j * ff:(j + 1) * ff, :])
    y_ref[...] = h
    h = y_ref[...]
    gate = _sigmoid(_dot(rms(h, gp_ref[...]), wg_ref[...]))
    y_ref[...] = h + gate * _dot(p_ref[...].astype(BF16), wp_ref[...])


def _post(x2d, oa, ob, p2d, w_out, g_mlp, w_up, w_down, g_ple, w_gate, w_proj, tm):
    n = x2d.shape[0]
    row = lambda w: pl.BlockSpec((tm, w), lambda i: (i, 0))
    cs = lambda a: pl.BlockSpec(a.shape, lambda i: (0, 0), pipeline_mode=pl.Buffered(1))
    return pl.pallas_call(
        _post_body,
        grid=(n // tm,),
        in_specs=[row(D_MODEL), row(WIDTH), row(WIDTH), row(PLE_DIM), cs(w_out), cs(g_mlp), cs(w_up), cs(w_down),
                  cs(g_ple), cs(w_gate), cs(w_proj)],
        out_specs=row(D_MODEL),
        out_shape=jax.ShapeDtypeStruct((n, D_MODEL), F32),
        compiler_params=_params(("parallel",)),
        name="post",
    )(x2d, oa, ob, p2d, w_out, g_mlp, w_up, w_down, g_ple, w_gate, w_proj)


def _expand_matrix():
    lane = jnp.arange(2 * WIDTH)
    src = (lane // WIDTH) * N_HEADS + (lane % WIDTH) // HEAD_DIM
    return (jnp.arange(GATE_PAD)[:, None] == src[None, :]).astype(BF16)


def _seq_minor(a):
    b, sq, h, d = a.shape
    return a.transpose(0, 2, 3, 1).reshape(b, h * d, sq)


def _group(x, p, conv_state, s0, past_k, past_v, wts, tm, sb_tk):
    batch, seq, _ = x.shape
    n = batch * seq
    x2d = x.reshape(n, D_MODEL)
    qkv, z, ab, q16, k, v, k16, v16 = _inproj(x2d, wts["g_mix"], wts["w_main"], wts["w_ab"], wts["bd_w"],
                                              wts["gq"], wts["gk"], tm, seq)
    prep = _dn_prep(qkv, ab, conv_state, wts, batch, seq)
    o_a, s_new = _dn_scan(prep, z, s0, wts, batch, seq)
    new_conv = qkv.reshape(batch, seq, CONV_DIM)[:, seq - (CONV_K - 1):, :]
    if past_k is None:
        o_b = _sb_prompt(q16, k16, v16, batch, seq)
    else:
        o_b = _sb_sample(q16, k16, v16, past_k, past_v, batch, seq, sb_tk)
    y = _post(x2d, o_a, o_b, p.reshape(n, PLE_DIM), wts["w_out"], wts["g_mlp"], wts["w_up"], wts["w_down"],
              wts["g_ple"], wts["w_gate"], wts["w_proj"], min(POST_ROWS, n))
    if k.ndim == 3:
        heads = lambda a: a.reshape(batch, N_HEADS, HEAD_DIM, seq).transpose(0, 3, 1, 2)
    else:
        heads = lambda a: a.reshape(batch, seq, N_HEADS, HEAD_DIM)
    return y.reshape(batch, seq, D_MODEL), new_conv, s_new, heads(k), heads(v)


def _layer_weights(i, g_mix, w_in, conv_w, a_log, dt_bias, g_out_dn, g_q_sb, g_k_sb, w_out, g_mlp, w_up, w_down,
                   g_ple, w_ple_gate, w_ple_proj):
    w = w_in[i]
    gate0 = CONV_DIM + WIDTH
    sb0 = gate0 + 2 * N_HEADS
    pad_row = lambda a: jnp.pad(a.astype(F32), (0, GATE_PAD - a.shape[0])).reshape(1, GATE_PAD)
    return {
        "g_mix": g_mix[i].reshape(1, D_MODEL),
        "w_main": jnp.concatenate([w[:, :gate0], w[:, sb0:]], axis=1).astype(BF16),
        "w_ab": jnp.pad(w[:, gate0:sb0], ((0, 0), (0, GATE_PAD - 2 * N_HEADS))).astype(BF16),
        "bd_w": _head_blockdiag(WIDTH, HEAD_DIM),
        "expand": _expand_matrix(),
        "gq": jnp.tile(g_q_sb[i], N_HEADS).reshape(1, WIDTH),
        "gk": jnp.tile(g_k_sb[i], N_HEADS).reshape(1, WIDTH),
        "conv_w": conv_w[i],
        "alog": pad_row(a_log[i]),
        "dtb": pad_row(dt_bias[i]),
        "gout": jnp.tile(g_out_dn[i], N_HEADS).reshape(1, WIDTH),
        "w_out": w_out[i].astype(BF16),
        "g_mlp": g_mlp[i].reshape(1, D_MODEL),
        "w_up": w_up[i].astype(BF16),
        "w_down": w_down[i].astype(BF16),
        "g_ple": g_ple[i].reshape(1, D_MODEL),
        "w_gate": w_ple_gate[i].astype(BF16),
        "w_proj": w_ple_proj[i].astype(BF16),
    }


def kernel(x_prompt, x_sample, cache_conv, state_delta, cache_k, cache_v, p_prompt, p_sample, g_mix, w_in, conv_w, a_log, dt_bias, g_out_dn, g_q_sb, g_k_sb, w_out, g_mlp, w_up, w_down, g_ple, w_ple_gate, w_ple_proj):
    depth = w_in.shape[0]
    bp = x_prompt.shape[0]
    y_p, y_s = x_prompt, x_sample
    outs_p, outs_s = [], []
    for i in range(depth):
        wts = _layer_weights(i, g_mix, w_in, conv_w, a_log, dt_bias, g_out_dn, g_q_sb, g_k_sb, w_out, g_mlp,
                             w_up, w_down, g_ple, w_ple_gate, w_ple_proj)
        y_p, *rest = _group(y_p, p_prompt[i], jnp.zeros((bp, CONV_K - 1, CONV_DIM), F32),
                            jnp.zeros((bp, N_HEADS, HEAD_DIM, HEAD_DIM), F32), None, None, wts,
                            tm=512, sb_tk=None)
        outs_p.append(rest)
        dec_b, past_len = cache_k.shape[1], cache_k.shape[2]
        y_s, *rest = _group(y_s, p_sample[i], cache_conv[i], state_delta[i],
                            _seq_minor(cache_k[i]), _seq_minor(cache_v[i]),
                            wts, tm=512, sb_tk=min(SB_BLOCK, past_len))
        outs_s.append(rest)
    stack = lambda outs, j: jnp.stack([o[j] for o in outs])
    return (y_p, y_s,
            stack(outs_p, 0), stack(outs_p, 1), stack(outs_p, 2), stack(outs_p, 3),
            stack(outs_s, 0), stack(outs_s, 1), stack(outs_s, 2), stack(outs_s, 3))
```
